```python
import math
import jax, jax.numpy as jnp
from jax import lax
import numpy as np

D_MODEL = 1024
BATCH = 8
SEQ = 8192
DEPTH = 1
DEC_BATCH = 8
DEC_SEQ = 2048
PAST_LEN = 128

N_MEM = 256
HEAD_DIM = 64
D_HY = 384
N_ATT_HEADS = 6
D_ATT = N_ATT_HEADS * HEAD_DIM
N_MEM_HEADS = 4
D_MEM = N_MEM_HEADS * HEAD_DIM
D_MIX = D_HY + D_ATT + D_MEM
D_IN = 3 * D_HY + 3 * D_ATT + D_MEM + D_MIX
FILTER_BANDS = 16
FILTER_EMB = 1 + 2 * FILTER_BANDS
FILTER_HIDDEN = 64
DECAY_TARGET = 1e-2
FAST_DECAY_PCT = 0.3
SLOW_DECAY_PCT = 1.5
DILATED_CONFIGS = ((128, 1), (512, 4), (2048, 16))
N_BUCKETS = 32
MAX_DISTANCE = 1024
RMS_EPS = 1e-6
NEG_INF = -1e30

kernel_name = 'hymba_hyena_dilated_memxattn_encoder'


def rms_norm(x, g):
    x32 = x.astype(jnp.float32)
    y = x32 * lax.rsqrt(jnp.mean(x32 * x32, axis=-1, keepdims=True) + RMS_EPS) * g.astype(jnp.float32)
    return y.astype(x.dtype)


def head_rms(x, g):
    x32 = x.astype(jnp.float32)
    return x32 * lax.rsqrt(jnp.mean(x32 * x32, axis=-1, keepdims=True) + RMS_EPS) * g.astype(jnp.float32)


def short_conv3(z, w, b):
    zp = jnp.pad(z, ((0, 0), (1, 1), (0, 0)))
    return zp[:, :-2] * w[0] + zp[:, 1:-1] * w[1] + zp[:, 2:] * w[2] + b


def hyena_filter(L, w1, b1, freq, w2, b2, w3):
    f32 = jnp.float32
    pos = jnp.arange(L, dtype=f32)[:, None]
    t = pos / max(L - 1, 1)
    bands = jnp.linspace(1e-4, FILTER_BANDS - 1, FILTER_BANDS, dtype=f32)[None, :]
    ang = (2.0 * math.pi / L) * pos * bands
    feats = jnp.concatenate([t, jnp.cos(ang), -jnp.sin(ang)], axis=-1)
    fr = freq.astype(f32)
    h = jnp.sin(fr * (feats @ w1.astype(f32) + b1.astype(f32)))
    h = jnp.sin(fr * (h @ w2.astype(f32) + b2.astype(f32)))
    h = (h @ w3.astype(f32)).reshape(L, 2, D_HY)
    deltas = jnp.abs(jnp.linspace(math.log(DECAY_TARGET) / SLOW_DECAY_PCT,
                                  math.log(DECAY_TARGET) / FAST_DECAY_PCT, D_HY, dtype=f32))
    h = h * jnp.exp(-t * deltas)[:, None, :]
    fwd, bwd = h[:, 0], h[:, 1]
    k = jnp.concatenate([fwd[:1] + bwd[:1], fwd[1:], jnp.zeros((1, D_HY), f32), bwd[:0:-1]], axis=0)
    return k * lax.rsqrt(jnp.sum(k * k, axis=0, keepdims=True) + 1e-12)


def hyena_branch(z_hy, conv_w, conv_b, w1, b1, freq, w2, b2, w3, skip):
    B, L, _ = z_hy.shape
    u = short_conv3(z_hy.astype(jnp.float32), conv_w.astype(jnp.float32), conv_b.astype(jnp.float32))
    x0, x1, v = jnp.split(u, 3, axis=-1)
    s = x1 * v
    k = hyena_filter(L, w1, b1, freq, w2, b2, w3)
    S = jnp.fft.rfft(s, n=2 * L, axis=1)
    K = jnp.fft.rfft(k, axis=0)
    conv = jnp.fft.irfft(S * K[None], n=2 * L, axis=1)[:, :L]
    return x0 * (conv + s * skip.astype(jnp.float32))


def t5_bucket(rel):
    half = N_BUCKETS // 2
    max_exact = half // 2
    ret = jnp.where(rel > 0, half, 0)
    n = jnp.abs(rel)
    large = max_exact + (jnp.log(jnp.maximum(n, 1).astype(jnp.float32) / max_exact)
                         / math.log(MAX_DISTANCE / max_exact) * (half - max_exact)).astype(jnp.int32)
    large = jnp.minimum(large, half - 1)
    return ret + jnp.where(n < max_exact, n, large)


def dilated_attention(q, k, v, rel_bias):
    B, L, H, E = q.shape
    scale = E ** -0.5
    outs, lses = [], []
    for window, dil in DILATED_CONFIGS:
        R = window // (2 * dil)
        blk = R
        n = L // dil
        nblk = -(-n // blk)
        pad = nblk * blk - n

        def to_blocks(a):
            a = a.reshape(B, n, dil, H, E).transpose(0, 2, 1, 3, 4)
            a = jnp.pad(a, ((0, 0), (0, 0), (0, pad), (0, 0), (0, 0)))
            return a.reshape(B, dil, nblk, blk, H, E)

        def band(a):
            ap = jnp.pad(a, ((0, 0), (0, 0), (1, 1), (0, 0), (0, 0), (0, 0)))
            return jnp.concatenate([ap[:, :, :-2], ap[:, :, 1:-1], ap[:, :, 2:]], axis=3)

        def from_blocks(a):
            a = a.reshape((B, dil, nblk * blk) + a.shape[4:])[:, :, :n]
            a = jnp.moveaxis(a, 1, 2)
            return a.reshape((B, L) + a.shape[3:])

        qb = to_blocks(q)
        kb = band(to_blocks(k))
        vb = band(to_blocks(v))
        qi = jnp.arange(blk)[:, None]
        kj = jnp.arange(3 * blk)[None, :]
        rel = kj - blk - qi
        kpos = jnp.arange(nblk)[:, None, None] * blk - blk + kj[None]
        valid = (jnp.abs(rel) <= R)[None] & (kpos >= 0) & (kpos < n)
        bias = rel_bias[t5_bucket(rel * dil)].astype(jnp.float32).transpose(2, 0, 1)
        logits = jnp.einsum('bdnqhe,bdnkhe->bdnhqk', qb, kb) * scale + bias
        logits = jnp.where(valid[None, None, :, None], logits, NEG_INF)
        m = jnp.max(logits, axis=-1, keepdims=True)
        p = jnp.exp(logits - m)
        s = jnp.sum(p, axis=-1, keepdims=True)
        o = jnp.einsum('bdnhqk,bdnkhe->bdnqhe', p, vb) / jnp.swapaxes(s, 3, 4)
        lse = jnp.swapaxes((m + jnp.log(s))[..., 0], 3, 4)
        outs.append(from_blocks(o))
        lses.append(from_blocks(lse))
    wts = jax.nn.softmax(jnp.stack(lses, axis=0), axis=0)
    return jnp.sum(wts[..., None] * jnp.stack(outs, axis=0), axis=0)


def memory_attention(qm, mem, mem_norm, w_mem_kv, k_gain):
    B, M, _ = mem.shape
    kv = (rms_norm(mem, mem_norm) @ w_mem_kv).reshape(B, M, 2, N_MEM_HEADS, HEAD_DIM)
    km = head_rms(kv[:, :, 0], k_gain)
    vm = kv[:, :, 1].astype(jnp.float32)
    logits = jnp.einsum('blhe,bmhe->bhlm', qm, km) * (HEAD_DIM ** -0.5)
    p = jax.nn.softmax(logits, axis=-1)
    return jnp.einsum('bhlm,bmhe->blhe', p, vm)


def encoder_layer(x, mem, norm_in, w_in, hy_conv_w, hy_conv_b, hy_filt_w1, hy_filt_b1, hy_filt_freq,
                  hy_filt_w2, hy_filt_b2, hy_filt_w3, hy_skip, att_q_norm, att_k_norm,
                  mem_norm, w_mem_kv, mem_q_norm, mem_k_norm, w_out, rel_bias):
    B, L, _ = x.shape
    h = rms_norm(x, norm_in)
    z = h @ w_in
    i0 = 3 * D_HY
    i1 = i0 + 3 * D_ATT
    i2 = i1 + D_MEM
    z_hy, z_att, z_mq, gate = z[..., :i0], z[..., i0:i1], z[..., i1:i2], z[..., i2:]
    y_hy = hyena_branch(z_hy, hy_conv_w, hy_conv_b, hy_filt_w1, hy_filt_b1, hy_filt_freq,
                        hy_filt_w2, hy_filt_b2, hy_filt_w3, hy_skip)
    qkv = z_att.reshape(B, L, 3, N_ATT_HEADS, HEAD_DIM)
    q = head_rms(qkv[:, :, 0], att_q_norm)
    k = head_rms(qkv[:, :, 1], att_k_norm)
    v = qkv[:, :, 2].astype(jnp.float32)
    y_att = dilated_attention(q, k, v, rel_bias).reshape(B, L, D_ATT)
    qm = head_rms(z_mq.reshape(B, L, N_MEM_HEADS, HEAD_DIM), mem_q_norm)
    y_mem = memory_attention(qm, mem, mem_norm, w_mem_kv, mem_k_norm).reshape(B, L, D_MEM)
    mixed = jnp.concatenate([y_hy, y_att, y_mem], axis=-1) * jax.nn.silu(gate.astype(jnp.float32))
    return x + mixed.astype(x.dtype) @ w_out


def setup_inputs(seed: int = 0) -> dict:
    key = jax.random.key(seed)
    ks = jax.random.split(key, 24)
    f32 = jnp.float32
    nrm = lambda k, shape, s: jax.random.normal(k, shape, f32) * s
    gain = lambda k, shape: 1.0 + 0.02 * jax.random.normal(k, shape, f32)
    return {
        'x_prompt': jax.random.normal(ks[0], (BATCH, SEQ, D_MODEL), f32),
        'x_sample': jax.random.normal(ks[1], (DEC_BATCH, DEC_SEQ, D_MODEL), f32),
        'mem_prompt': jax.random.normal(ks[2], (BATCH, N_MEM, D_MODEL), f32),
        'mem_sample': jax.random.normal(ks[3], (DEC_BATCH, N_MEM, D_MODEL), f32),
        'norm_in': gain(ks[4], (DEPTH, D_MODEL)),
        'w_in': nrm(ks[5], (DEPTH, D_MODEL, D_IN), D_MODEL ** -0.5),
        'hy_conv_w': nrm(ks[6], (DEPTH, 3, 3 * D_HY), 3 ** -0.5),
        'hy_conv_b': nrm(ks[7], (DEPTH, 3 * D_HY), 0.02),
        'hy_filt_w1': nrm(ks[8], (DEPTH, FILTER_EMB, FILTER_HIDDEN), FILTER_EMB ** -0.5),
        'hy_filt_b1': nrm(ks[9], (DEPTH, FILTER_HIDDEN), 0.02),
        'hy_filt_freq': gain(ks[10], (DEPTH, FILTER_HIDDEN)),
        'hy_filt_w2': nrm(ks[11], (DEPTH, FILTER_HIDDEN, FILTER_HIDDEN), FILTER_HIDDEN ** -0.5),
        'hy_filt_b2': nrm(ks[12], (DEPTH, FILTER_HIDDEN), 0.02),
        'hy_filt_w3': nrm(ks[13], (DEPTH, FILTER_HIDDEN, 2 * D_HY), FILTER_HIDDEN ** -0.5),
        'hy_skip': nrm(ks[14], (DEPTH, D_HY), 1.0),
        'att_q_norm': gain(ks[15], (DEPTH, HEAD_DIM)),
        'att_k_norm': gain(ks[16], (DEPTH, HEAD_DIM)),
        'mem_norm': gain(ks[17], (DEPTH, D_MODEL)),
        'w_mem_kv': nrm(ks[18], (DEPTH, D_MODEL, 2 * D_MEM), D_MODEL ** -0.5),
        'mem_q_norm': gain(ks[19], (DEPTH, HEAD_DIM)),
        'mem_k_norm': gain(ks[20], (DEPTH, HEAD_DIM)),
        'w_out': nrm(ks[21], (DEPTH, D_MIX, D_MODEL), D_MIX ** -0.5),
        'rel_bias': nrm(ks[22], (N_BUCKETS, N_ATT_HEADS), 0.1),
    }


def reference(x_prompt, x_sample, mem_prompt, mem_sample, norm_in, w_in, hy_conv_w, hy_conv_b,
              hy_filt_w1, hy_filt_b1, hy_filt_freq, hy_filt_w2, hy_filt_b2, hy_filt_w3, hy_skip,
              att_q_norm, att_k_norm, mem_norm, w_mem_kv, mem_q_norm, mem_k_norm, w_out, rel_bias):
    def trunk(x, mem):
        for l in range(DEPTH):
            x = encoder_layer(x, mem, norm_in[l], w_in[l], hy_conv_w[l], hy_conv_b[l],
                              hy_filt_w1[l], hy_filt_b1[l], hy_filt_freq[l], hy_filt_w2[l],
                              hy_filt_b2[l], hy_filt_w3[l], hy_skip[l], att_q_norm[l], att_k_norm[l],
                              mem_norm[l], w_mem_kv[l], mem_q_norm[l], mem_k_norm[l], w_out[l], rel_bias)
        return x

    y_prompt = trunk(x_prompt, mem_prompt)
    y_sample = trunk(x_sample, mem_sample)
    return (y_prompt, y_sample)
```

```python
import functools
import math

import numpy as np
import jax
import jax.numpy as jnp
from jax import lax
from jax.experimental import pallas as pl
from jax.experimental.pallas import tpu as pltpu

F32 = jnp.float32
BF16 = jnp.bfloat16

N_MEM = 256
HEAD_DIM = 64
D_HY = 384
N_ATT_HEADS = 6
D_ATT = N_ATT_HEADS * HEAD_DIM
N_MEM_HEADS = 4
D_MEM = N_MEM_HEADS * HEAD_DIM
D_MIX = D_HY + D_ATT + D_MEM
FILTER_BANDS = 16
FILTER_HIDDEN = 64
DECAY_TARGET = 1e-2
FAST_DECAY_PCT = 0.3
SLOW_DECAY_PCT = 1.5
DILATED_CONFIGS = ((128, 1), (512, 4), (2048, 16))
N_BUCKETS = 32
MAX_DISTANCE = 1024
RMS_EPS = 1e-6
NEG_INF = -1e30

LANES = 128
MXU_EDGE = 256

TOKEN_TILE = 512
SEQ_BLOCK = MXU_EDGE
Q_BLOCK = 128
ATT_SIDE = 64
FILTER_TILE = 2048
ROLL_CHUNK = 1024
FEAT_ROWS = 40
VMEM_LIMIT = 56 * 1024 * 1024


def _params(n_axes):
    return pltpu.CompilerParams(dimension_semantics=("arbitrary",) * n_axes,
                                vmem_limit_bytes=VMEM_LIMIT)


def _head_mean_square(z, bd):
    return jnp.dot((z * z).astype(BF16), bd, preferred_element_type=F32)


def _in_proj_kernel(x_ref, g_ref, w_ref, bd_ref, qkg_ref, mqg_ref,
                    zhy_ref, qkv_ref, qm_ref, gs_ref):
    i0 = 3 * D_HY
    i1 = i0 + 3 * D_ATT
    i2 = i1 + D_MEM
    x = x_ref[...]
    ms = jnp.mean(x * x, axis=-1, keepdims=True)
    h = (x * lax.rsqrt(ms + RMS_EPS) * g_ref[...]).astype(BF16)
    zhy_ref[...] = jnp.dot(h, w_ref[:, 0:i0], preferred_element_type=F32).astype(BF16)
    bd = bd_ref[...]
    za = jnp.dot(h, w_ref[:, i0:i1], preferred_element_type=F32)
    for c in range(2 * D_ATT // MXU_EDGE):
        sl = slice(c * MXU_EDGE, (c + 1) * MXU_EDGE)
        z = za[:, sl]
        zn = z * lax.rsqrt(_head_mean_square(z, bd) + RMS_EPS) * qkg_ref[:, sl]
        qkv_ref[:, sl] = zn.astype(BF16)
    qkv_ref[:, 2 * D_ATT:] = za[:, 2 * D_ATT:].astype(BF16)
    zq = jnp.dot(h, w_ref[:, i1:i2], preferred_element_type=F32)
    qm_ref[...] = (zq * lax.rsqrt(_head_mean_square(zq, bd) + RMS_EPS) * mqg_ref[...]).astype(BF16)
    gate = jnp.dot(h, w_ref[:, i2:], preferred_element_type=F32)
    gs_ref[...] = (gate * jax.nn.sigmoid(gate)).astype(BF16)


def _in_proj(x2d, g, w, bd, qkg, mqg):
    n, d = x2d.shape
    d_in = w.shape[1]
    tm = TOKEN_TILE
    const = lambda i: (0, 0)
    row = lambda i: (i, 0)
    return pl.pallas_call(
        _in_proj_kernel,
        grid=(n // tm,),
        in_specs=[pl.BlockSpec((tm, d), row),
                  pl.BlockSpec((1, d), const),
                  pl.BlockSpec((d, d_in), const),
                  pl.BlockSpec((MXU_EDGE, MXU_EDGE), const),
                  pl.BlockSpec((1, 2 * D_ATT), const),
                  pl.BlockSpec((1, D_MEM), const)],
        out_specs=[pl.BlockSpec((tm, 3 * D_HY), row),
                   pl.BlockSpec((tm, 3 * D_ATT), row),
                   pl.BlockSpec((tm, D_MEM), row),
                   pl.BlockSpec((tm, D_MIX), row)],
        out_shape=[jax.ShapeDtypeStruct((n, 3 * D_HY), BF16),
                   jax.ShapeDtypeStruct((n, 3 * D_ATT), BF16),
                   jax.ShapeDtypeStruct((n, D_MEM), BF16),
                   jax.ShapeDtypeStruct((n, D_MIX), BF16)],
        compiler_params=_params(1),
        name="in_proj",
    )(x2d, g, w, bd, qkg, mqg)


def _hyena_pre_kernel(z_ref, zp_ref, zn_ref, cw_ref, cb_ref, skip_ref,
                      st_ref, x0_ref, p_ref, *, n_tiles):
    i = pl.program_id(1)
    t = z_ref.shape[0]
    z = z_ref[...].astype(F32)
    halo = zp_ref.shape[0]
    z_before = zp_ref[...].astype(F32)[halo - 1:halo, :] * (i > 0).astype(F32)
    z_after = zn_ref[...].astype(F32)[0:1, :] * (i < n_tiles - 1).astype(F32)
    row = lax.broadcasted_iota(jnp.int32, z.shape, 0)
    zm1 = jnp.where(row == 0, z_before, pltpu.roll(z, 1, axis=0))
    zp1 = jnp.where(row == t - 1, z_after, pltpu.roll(z, t - 1, axis=0))
    u = zm1 * cw_ref[0:1, :] + z * cw_ref[1:2, :] + zp1 * cw_ref[2:3, :] + cb_ref[...]
    x0 = u[:, 0:D_HY]
    s = u[:, D_HY:2 * D_HY] * u[:, 2 * D_HY:]
    x0_ref[...] = x0.astype(BF16)
    p_ref[...] = (x0 * (s * skip_ref[...])).astype(BF16)
    st_ref[...] = s.T.astype(BF16)


def _hyena_pre(zhy, cw, cb, skip, batch, seq):
    n = zhy.shape[0]
    t = SEQ_BLOCK
    nb = seq // t
    halo = 16
    per = t // halo
    main = lambda b, i: (b * nb + i, 0)
    prev = lambda b, i: (jnp.maximum((b * nb + i) * per - 1, 0), 0)
    nxt = lambda b, i: (jnp.minimum((b * nb + i + 1) * per, n // halo - 1), 0)
    const = lambda b, i: (0, 0)
    return pl.pallas_call(
        functools.partial(_hyena_pre_kernel, n_tiles=nb),
        grid=(batch, nb),
        in_specs=[pl.BlockSpec((t, 3 * D_HY), main),
                  pl.BlockSpec((halo, 3 * D_HY), prev),
                  pl.BlockSpec((halo, 3 * D_HY), nxt),
                  pl.BlockSpec((3, 3 * D_HY), const),
                  pl.BlockSpec((1, 3 * D_HY), const),
                  pl.BlockSpec((1, D_HY), const)],
        out_specs=[pl.BlockSpec((D_HY, t), lambda b, i: (0, i * batch + b)),
                   pl.BlockSpec((t, D_HY), main),
                   pl.BlockSpec((t, D_HY), main)],
        out_shape=[jax.ShapeDtypeStruct((D_HY, nb * batch * t), BF16),
                   jax.ShapeDtypeStruct((n, D_HY), BF16),
                   jax.ShapeDtypeStruct((n, D_HY), BF16)],
        compiler_params=_params(2),
        name="hyena_pre",
    )(zhy, zhy, zhy, cw, cb, skip)


def _hyena_filter_kernel(band_ref, w1t_ref, b1_ref, fr_ref, w2t_ref, b2_ref, w3f_ref, w3b_ref,
                         dl_ref, g_ref, nrm_ref, ss_ref, *, seq, n_tiles):
    i = pl.program_id(0)
    tl = g_ref.shape[1]
    hi = lax.Precision.HIGHEST
    m = i * tl + lax.broadcasted_iota(jnp.int32, (1, tl), 1)
    lag = m - seq
    pos = jnp.abs(lag).astype(F32)
    tt = pos / float(max(seq - 1, 1))
    ang = ((2.0 * math.pi / seq) * pos) * band_ref[...]
    frow = lax.broadcasted_iota(jnp.int32, (FEAT_ROWS, tl), 0)
    feats = jnp.where(frow == 0, tt,
                      jnp.where(frow <= FILTER_BANDS, jnp.cos(ang),
                                jnp.where(frow <= 2 * FILTER_BANDS, -jnp.sin(ang), 0.0)))
    fr = fr_ref[...]
    h = jnp.sin(fr * (jnp.dot(w1t_ref[...], feats, precision=hi, preferred_element_type=F32)
                      + b1_ref[...]))
    h = jnp.sin(fr * (jnp.dot(w2t_ref[...], h, precision=hi, preferred_element_type=F32)
                      + b2_ref[...]))
    fwd = jnp.dot(w3f_ref[...], h, precision=hi, preferred_element_type=F32)
    bwd = jnp.dot(w3b_ref[...], h, precision=hi, preferred_element_type=F32)
    decay = jnp.exp(-tt * dl_ref[...])
    val = jnp.where(lag > 0, fwd, jnp.where(lag < 0, bwd, fwd + bwd)) * decay
    val = jnp.where(m == 0, 0.0, val)
    g_ref[...] = val

    @pl.when(i == 0)
    def _():
        ss_ref[...] = jnp.zeros_like(ss_ref)

    ss_ref[...] += jnp.sum(val * val, axis=1, keepdims=True)

    @pl.when(i == n_tiles - 1)
    def _():
        nrm_ref[...] = lax.rsqrt(ss_ref[...] + 1e-12)


def _hyena_filter(seq, band, w1t, b1, fr, w2t, b2, w3f, w3b, deltas):
    tl = min(FILTER_TILE, 2 * seq)
    n_tiles = 2 * seq // tl
    const = lambda i: (0, 0)
    full = lambda a: pl.BlockSpec(a.shape, const)
    args = (band, w1t, b1, fr, w2t, b2, w3f, w3b, deltas)
    return pl.pallas_call(
        functools.partial(_hyena_filter_kernel, seq=seq, n_tiles=n_tiles),
        grid=(n_tiles,),
        in_specs=[full(a) for a in args],
        out_specs=[pl.BlockSpec((D_HY, tl), lambda i: (0, i)),
                   pl.BlockSpec((D_HY, 1), const)],
        out_shape=[jax.ShapeDtypeStruct((D_HY, 2 * seq), F32),
                   jax.ShapeDtypeStruct((D_HY, 1), F32)],
        scratch_shapes=[pltpu.VMEM((D_HY, 1), F32)],
        compiler_params=_params(1),
        name="hyena_filter",
    )(*args)


def _hyena_conv_kernel(a_ref, g_ref, o_ref, e_ref, acc_ref, *, batch, nb):
    t = SEQ_BLOCK
    seq = nb * t
    width = 2 * seq
    half = t // 2
    chunk = min(ROLL_CHUNK, width)
    for c in range(width // chunk):
        m0 = c * chunk
        lo = max(m0 - half, 0)
        win = g_ref[:, lo:m0 + chunk]
        rolled = pltpu.roll(jnp.broadcast_to(win, (half, win.shape[1])), 0, axis=1,
                            stride=1, stride_axis=0)
        blk = rolled[:, m0 - lo:].astype(BF16)
        e_ref[0:half, m0:m0 + chunk] = blk
        hi = min(m0 + half + chunk, width)
        e_ref[half:t, m0 + half:hi] = blk[:, 0:hi - m0 - half]
    acc_ref[...] = jnp.zeros_like(acc_ref)
    for d in range(-(nb - 1), nb):
        base = seq + t * d
        if d >= 0:
            j0 = 0
            nj = nb - d
            nj += nj % 2
        else:
            j0 = -d - ((-d) % 2)
            nj = nb - j0
        rows = batch * nj
        r_in = batch * j0
        r_out = batch + batch * (j0 + d)
        acc_ref[r_out:r_out + rows, :] += jnp.dot(
            a_ref[r_in:r_in + rows, :], e_ref[:, base:base + t], preferred_element_type=F32)
    o_ref[...] = acc_ref[batch:batch + batch * nb, :]


def _hyena_conv(st3, g3, batch, nb):
    t = SEQ_BLOCK
    rows = nb * batch
    return pl.pallas_call(
        functools.partial(_hyena_conv_kernel, batch=batch, nb=nb),
        grid=(D_HY,),
        in_specs=[pl.BlockSpec((None, rows, t), lambda c: (c, 0, 0)),
                  pl.BlockSpec((None, 1, 2 * nb * t), lambda c: (c, 0, 0))],
        out_specs=pl.BlockSpec((None, rows, t), lambda c: (c, 0, 0)),
        out_shape=jax.ShapeDtypeStruct((D_HY, rows, t), F32),
        scratch_shapes=[pltpu.VMEM((t, 2 * nb * t), BF16),
                        pltpu.VMEM((rows + 2 * batch, t), F32)],
        compiler_params=_params(1),
        name="hyena_conv",
    )(st3, g3)


def _dil_attn_kernel(bucket_ref, relb_ref, q_ref, k_ref, v_ref, kp_ref, vp_ref, kn_ref, vn_ref,
                     o_ref, lse_ref, kall_ref, vall_ref, bias_ref, *, n):
    tq = q_ref.shape[0]
    side = ATT_SIDE
    first = (pl.program_id(0) == 0) & (pl.program_id(1) == 0) & (pl.program_id(2) == 0)

    @pl.when(first)
    def _():
        bucket = bucket_ref[...]
        for h in range(N_ATT_HEADS):
            bias_ref[h] = jnp.full(bucket.shape, NEG_INF, F32)

        def body(bk, carry):
            hit = bucket == bk
            for h in range(N_ATT_HEADS):
                bias_ref[h] = jnp.where(hit, relb_ref[bk * N_ATT_HEADS + h], bias_ref[h])
            return carry

        lax.fori_loop(0, N_BUCKETS, body, 0)

    kall_ref[0:side, :] = kp_ref[...]
    kall_ref[side:side + tq, :] = k_ref[...]
    kall_ref[side + tq:, :] = kn_ref[...]
    vall_ref[0:side, :] = vp_ref[...]
    vall_ref[side:side + tq, :] = v_ref[...]
    vall_ref[side + tq:, :] = vn_ref[...]

    i = pl.program_id(2)
    lane = lax.broadcasted_iota(jnp.int32, (Q_BLOCK, LANES), 1)
    low_head = lane < HEAD_DIM
    head_sel = [low_head[0:1, :].astype(BF16), jnp.logical_not(low_head)[0:1, :].astype(BF16)]
    for j in range(tq // Q_BLOCK):
        r0 = j * Q_BLOCK
        kpos = i * tq + r0 - side + lax.broadcasted_iota(jnp.int32, (1, 2 * Q_BLOCK), 1)
        in_seq = (kpos >= 0) & (kpos < n)
        for hp in range(N_ATT_HEADS // 2):
            cols = slice(hp * LANES, (hp + 1) * LANES)
            q2 = q_ref[r0:r0 + Q_BLOCK, cols]
            k2 = kall_ref[r0:r0 + 2 * Q_BLOCK, cols]
            v2 = vall_ref[r0:r0 + 2 * Q_BLOCK, cols]
            outs = []
            lses = []
            for hh in range(2):
                qh = q2 * head_sel[hh]
                s = lax.dot_general(qh, k2, (((1,), (1,)), ((), ())),
                                    preferred_element_type=F32)
                s = jnp.where(in_seq, s + bias_ref[2 * hp + hh], NEG_INF)
                mx = jnp.max(s, axis=-1, keepdims=True)
                p = jnp.exp(s - mx)
                den = jnp.sum(p, axis=-1, keepdims=True)
                o = jnp.dot(p.astype(BF16), v2, preferred_element_type=F32)
                outs.append(o * (1.0 / den))
                lses.append(jnp.broadcast_to(mx + jnp.log(den), o.shape))
            o_ref[r0:r0 + Q_BLOCK, cols] = jnp.where(low_head, outs[0], outs[1])
            lse_ref[r0:r0 + Q_BLOCK, cols] = jnp.where(low_head, lses[0], lses[1])


def _attn_bucket_table(dil):
    half = N_BUCKETS // 2
    max_exact = half // 2
    qi = np.arange(Q_BLOCK)[:, None]
    kj = np.arange(2 * Q_BLOCK)[None, :]
    rel_steps = kj - ATT_SIDE - qi
    rel = rel_steps * dil
    ret = np.where(rel > 0, half, 0)
    na = np.abs(rel)
    large = max_exact + (np.log(np.maximum(na, 1).astype(np.float32) / np.float32(max_exact))
                         / np.float32(math.log(MAX_DISTANCE / max_exact))
                         * np.float32(half - max_exact)).astype(np.int32)
    large = np.minimum(large, half - 1)
    bucket = ret + np.where(na < max_exact, na, large)
    return np.where(np.abs(rel_steps) <= ATT_SIDE, bucket, -1).astype(np.int32)


def _dil_attn(qkv, relb_flat, batch, seq, dil):
    n = seq // dil
    tq = min(n, 512)
    nt = n // tq
    side = ATT_SIDE
    per = tq // side
    view = qkv.reshape(batch, n, dil * 3 * D_ATT)
    bucket = jnp.asarray(_attn_bucket_table(dil))
    main = lambda c: (lambda b, r, i: (b, i, 3 * r + c))
    prev = lambda c: (lambda b, r, i: (b, jnp.maximum(i * per - 1, 0), 3 * r + c))
    nxt = lambda c: (lambda b, r, i: (b, jnp.minimum((i + 1) * per, n // side - 1), 3 * r + c))
    out_map = lambda b, r, i: (b, i, r)
    o, lse = pl.pallas_call(
        functools.partial(_dil_attn_kernel, n=n),
        grid=(batch, dil, nt),
        in_specs=[pl.BlockSpec(bucket.shape, lambda b, r, i: (0, 0)),
                  pl.BlockSpec(memory_space=pltpu.SMEM),
                  pl.BlockSpec((None, tq, D_ATT), main(0)),
                  pl.BlockSpec((None, tq, D_ATT), main(1)),
                  pl.BlockSpec((None, tq, D_ATT), main(2)),
                  pl.BlockSpec((None, side, D_ATT), prev(1)),
                  pl.BlockSpec((None, side, D_ATT), prev(2)),
                  pl.BlockSpec((None, side, D_ATT), nxt(1)),
                  pl.BlockSpec((None, side, D_ATT), nxt(2))],
        out_specs=[pl.BlockSpec((None, tq, D_ATT), out_map),
                   pl.BlockSpec((None, tq, D_ATT), out_map)],
        out_shape=[jax.ShapeDtypeStruct((batch, n, dil * D_ATT), F32),
                   jax.ShapeDtypeStruct((batch, n, dil * D_ATT), F32)],
        scratch_shapes=[pltpu.VMEM((tq + 2 * side, D_ATT), BF16),
                        pltpu.VMEM((tq + 2 * side, D_ATT), BF16),
                        pltpu.VMEM((N_ATT_HEADS, Q_BLOCK, 2 * Q_BLOCK), F32)],
        compiler_params=_params(3),
        name=f"dil_attn_d{dil}",
    )(bucket, relb_flat, view, view, view, view, view, view, view)
    return o.reshape(batch * seq, D_ATT), lse.reshape(batch * seq, D_ATT)


def _mem_kv_kernel(m_ref, g_ref, w_ref, bd_ref, kg_ref, kv_ref):
    x = m_ref[...]
    ms = jnp.mean(x * x, axis=-1, keepdims=True)
    h = (x * lax.rsqrt(ms + RMS_EPS) * g_ref[...]).astype(BF16)
    kv = jnp.dot(h, w_ref[...], preferred_element_type=F32)
    k = kv[:, 0:D_MEM]
    kn = k * lax.rsqrt(_head_mean_square(k, bd_ref[...]) + RMS_EPS) * kg_ref[...]
    kv_ref[:, 0:D_MEM] = kn.astype(BF16)
    kv_ref[:, D_MEM:] = kv[:, D_MEM:].astype(BF16)


def _mem_kv(mem, g, w, bd, kg):
    batch, n_mem, d = mem.shape
    const = lambda b: (0, 0)
    return pl.pallas_call(
        _mem_kv_kernel,
        grid=(batch,),
        in_specs=[pl.BlockSpec((None, n_mem, d), lambda b: (b, 0, 0)),
                  pl.BlockSpec((1, d), const),
                  pl.BlockSpec((d, 2 * D_MEM), const),
                  pl.BlockSpec((MXU_EDGE, MXU_EDGE), const),
                  pl.BlockSpec((1, D_MEM), const)],
        out_specs=pl.BlockSpec((None, n_mem, 2 * D_MEM), lambda b: (b, 0, 0)),
        out_shape=jax.ShapeDtypeStruct((batch, n_mem, 2 * D_MEM), BF16),
        compiler_params=_params(1),
        name="mem_kv",
    )(mem, g, w, bd, kg)


def _out_proj_kernel(x_ref, ct_ref, nrm_ref, x0_ref, p_ref, o1_ref, l1_ref, o2_ref, l2_ref,
                     o3_ref, l3_ref, qm_ref, kv_ref, gs_ref, w_ref, y_ref):
    t = x_ref.shape[0]
    conv = (ct_ref[...] * nrm_ref[...]).T
    y_hy = x0_ref[...].astype(F32) * conv + p_ref[...].astype(F32)

    l1, l2, l3 = l1_ref[...], l2_ref[...], l3_ref[...]
    top = jnp.maximum(jnp.maximum(l1, l2), l3)
    e1, e2, e3 = jnp.exp(l1 - top), jnp.exp(l2 - top), jnp.exp(l3 - top)
    y_att = (e1 * o1_ref[...] + e2 * o2_ref[...] + e3 * o3_ref[...]) / (e1 + e2 + e3)

    lane = lax.broadcasted_iota(jnp.int32, (t, LANES), 1)
    low_head = lane < HEAD_DIM
    head_sel = [low_head[0:1, :].astype(BF16), jnp.logical_not(low_head)[0:1, :].astype(BF16)]
    mem_parts = []
    for hp in range(N_MEM_HEADS // 2):
        cols = slice(hp * LANES, (hp + 1) * LANES)
        q2 = qm_ref[:, cols]
        k2 = kv_ref[:, cols]
        v2 = kv_ref[:, D_MEM + hp * LANES:D_MEM + (hp + 1) * LANES]
        outs = []
        for hh in range(2):
            qh = q2 * head_sel[hh]
            s = lax.dot_general(qh, k2, (((1,), (1,)), ((), ())), preferred_element_type=F32)
            mx = jnp.max(s, axis=-1, keepdims=True)
            p = jnp.exp(s - mx)
            den = jnp.sum(p, axis=-1, keepdims=True)
            outs.append(jnp.dot(p.astype(BF16), v2, preferred_element_type=F32) * (1.0 / den))
        mem_parts.append(jnp.where(low_head, outs[0], outs[1]))
    mixed = jnp.concatenate([y_hy, y_att] + mem_parts, axis=-1) * gs_ref[...].astype(F32)
    y_ref[...] = x_ref[...] + jnp.dot(mixed.astype(BF16), w_ref[...], preferred_element_type=F32)


def _out_proj(x2d, conv_t, nrm, x0, p, att, qm, kv, gs, w, batch, seq):
    n, d = x2d.shape
    t = SEQ_BLOCK
    nb = seq // t
    main = lambda b, i: (b * nb + i, 0)
    const = lambda b, i: (0, 0)
    tok = lambda width: pl.BlockSpec((t, width), main)
    att_specs = [tok(D_ATT)] * 6
    att_args = [a for pair in att for a in pair]
    return pl.pallas_call(
        _out_proj_kernel,
        grid=(batch, nb),
        in_specs=[tok(d),
                  pl.BlockSpec((D_HY, t), lambda b, i: (0, i * batch + b)),
                  pl.BlockSpec((D_HY, 1), const),
                  tok(D_HY), tok(D_HY)] + att_specs + [
                  tok(D_MEM),
                  pl.BlockSpec((None, N_MEM, 2 * D_MEM), lambda b, i: (b, 0, 0)),
                  tok(D_MIX),
                  pl.BlockSpec((D_MIX, d), const)],
        out_specs=tok(d),
        out_shape=jax.ShapeDtypeStruct((n, d), F32),
        compiler_params=_params(2),
        name="out_proj",
    )(x2d, conv_t, nrm, x0, p, *att_args, qm, kv, gs, w)


def _encoder_layer(x, mem, norm_in, w_in, hy_conv_w, hy_conv_b, hy_filt_w1, hy_filt_b1,
                   hy_filt_freq, hy_filt_w2, hy_filt_b2, hy_filt_w3, hy_skip, att_q_norm,
                   att_k_norm, mem_norm, w_mem_kv, mem_q_norm, mem_k_norm, w_out, rel_bias):
    batch, seq, d = x.shape
    assert seq % (SEQ_BLOCK * 2) == 0 and (batch * seq) % TOKEN_TILE == 0
    assert all(seq // dil >= Q_BLOCK for _, dil in DILATED_CONFIGS)
    assert all(win // (2 * dil) == ATT_SIDE for win, dil in DILATED_CONFIGS)
    nb = seq // SEQ_BLOCK
    scale = HEAD_DIM ** -0.5

    blk = np.arange(MXU_EDGE) // HEAD_DIM
    bd = jnp.asarray((blk[:, None] == blk[None, :]).astype(np.float32) / HEAD_DIM, BF16)
    qkg = jnp.concatenate([jnp.tile(att_q_norm.astype(F32), N_ATT_HEADS) * scale,
                           jnp.tile(att_k_norm.astype(F32), N_ATT_HEADS)])[None, :]
    mqg = (jnp.tile(mem_q_norm.astype(F32), N_MEM_HEADS) * scale)[None, :]
    mkg = jnp.tile(mem_k_norm.astype(F32), N_MEM_HEADS)[None, :]

    x2d = x.reshape(batch * seq, d)
    zhy, qkv, qm, gs = _in_proj(x2d, norm_in[None, :], w_in.astype(BF16), bd, qkg, mqg)

    st, x0, p = _hyena_pre(zhy, hy_conv_w.astype(F32), hy_conv_b.astype(F32)[None, :],
                           hy_skip.astype(F32)[None, :], batch, seq)

    band = jnp.linspace(1e-4, FILTER_BANDS - 1, FILTER_BANDS, dtype=F32)
    band_col = jnp.concatenate([jnp.zeros((1,), F32), band, band,
                                jnp.zeros((FEAT_ROWS - 1 - 2 * FILTER_BANDS,), F32)])[:, None]
    w1t = jnp.pad(hy_filt_w1.astype(F32).T, ((0, 0), (0, FEAT_ROWS - hy_filt_w1.shape[0])))
    deltas = jnp.abs(jnp.linspace(math.log(DECAY_TARGET) / SLOW_DECAY_PCT,
                                  math.log(DECAY_TARGET) / FAST_DECAY_PCT, D_HY, dtype=F32))
    w3t = hy_filt_w3.astype(F32).T
    g, nrm = _hyena_filter(seq, band_col, w1t, hy_filt_b1.astype(F32)[:, None],
                           hy_filt_freq.astype(F32)[:, None], hy_filt_w2.astype(F32).T,
                           hy_filt_b2.astype(F32)[:, None], w3t[0:D_HY], w3t[D_HY:],
                           deltas[:, None])

    st3 = st.reshape(D_HY, nb * batch, SEQ_BLOCK)
    conv_t = _hyena_conv(st3, g.reshape(D_HY, 1, 2 * seq), batch, nb)
    conv_t = conv_t.reshape(D_HY, nb * batch * SEQ_BLOCK)

    relb_flat = rel_bias.astype(F32).reshape(-1)
    att = [_dil_attn(qkv, relb_flat, batch, seq, dil) for _, dil in DILATED_CONFIGS]

    kv = _mem_kv(mem, mem_norm[None, :], w_mem_kv.astype(BF16), bd, mkg)

    y = _out_proj(x2d, conv_t, nrm, x0, p, att, qm, kv, gs, w_out.astype(BF16), batch, seq)
    return y.reshape(batch, seq, d)


def kernel(x_prompt, x_sample, mem_prompt, mem_sample, norm_in, w_in, hy_conv_w, hy_conv_b, hy_filt_w1, hy_filt_b1, hy_filt_freq, hy_filt_w2, hy_filt_b2, hy_filt_w3, hy_skip, att_q_norm, att_k_norm, mem_norm, w_mem_kv, mem_q_norm, mem_k_norm, w_out, rel_bias):
    def trunk(x, mem):
        for l in range(norm_in.shape[0]):
            x = _encoder_layer(x, mem, norm_in[l], w_in[l], hy_conv_w[l], hy_conv_b[l],
                               hy_filt_w1[l], hy_filt_b1[l], hy_filt_freq[l], hy_filt_w2[l],
                               hy_filt_b2[l], hy_filt_w3[l], hy_skip[l], att_q_norm[l],
                               att_k_norm[l], mem_norm[l], w_mem_kv[l], mem_q_norm[l],
                               mem_k_norm[l], w_out[l], rel_bias)
        return x

    return (trunk(x_prompt, mem_prompt), trunk(x_sample, mem_sample))
```

```python
import functools
import math

import numpy as np
import jax
import jax.numpy as jnp
from jax import lax
from jax.experimental import pallas as pl
from jax.experimental.pallas import tpu as pltpu

F32 = jnp.float32
BF16 = jnp.bfloat16

N_MEM = 256
HEAD_DIM = 64
D_HY = 384
N_ATT_HEADS = 6
D_ATT = N_ATT_HEADS * HEAD_DIM
N_MEM_HEADS = 4
D_MEM = N_MEM_HEADS * HEAD_DIM
D_MIX = D_HY + D_ATT + D_MEM
FILTER_BANDS = 16
FILTER_HIDDEN = 64
DECAY_TARGET = 1e-2
FAST_DECAY_PCT = 0.3
SLOW_DECAY_PCT = 1.5
DILATED_CONFIGS = ((128, 1), (512, 4), (2048, 16))
N_BUCKETS = 32
MAX_DISTANCE = 1024
RMS_EPS = 1e-6
NEG_INF = -1e30

SUBLANES = 8
LANES = 128
MXU_EDGE = 256

SEQ_BLOCK = MXU_EDGE
TILE_BATCH = 2
Q_BLOCK = 128
ATT_SIDE = 64
FILTER_TILE = 2048
ROLL_CHUNK = 1024
CONV_CHANNELS = 8
FEAT_ROWS = 40
VMEM_LIMIT = 56 * 1024 * 1024


def _params(n_axes):
    return pltpu.CompilerParams(dimension_semantics=("arbitrary",) * n_axes,
                                vmem_limit_bytes=VMEM_LIMIT)


def _head_mean_square(z, bd):
    return jnp.dot((z * z).astype(BF16), bd, preferred_element_type=F32)


def _in_proj_kernel(x_ref, xp_ref, xn_ref, g_ref, w_ref, bd_ref, qkg_ref, mqg_ref,
                    cw_ref, cb_ref, skip_ref,
                    st_ref, x0_ref, p_ref, q1_ref, q4_ref, q16_ref, qm_ref, gs_ref,
                    h_ref, stage_ref, *, n_blocks):
    i0 = 3 * D_HY
    i1 = i0 + 3 * D_ATT
    i2 = i1 + D_MEM
    j = pl.program_id(1)
    nbat, t, d = x_ref.shape
    halo = xp_ref.shape[1]
    rows = nbat * t

    def normed(xv):
        ms = jnp.mean(xv * xv, axis=-1, keepdims=True)
        return (xv * lax.rsqrt(ms + RMS_EPS) * g_ref[...]).astype(BF16)

    h_ref[0:rows, :] = normed(x_ref[...].reshape(rows, d))
    h_ref[rows:rows + nbat * halo, :] = normed(xp_ref[...].reshape(nbat * halo, d))
    h_ref[rows + nbat * halo:, :] = normed(xn_ref[...].reshape(nbat * halo, d))

    zall = jnp.dot(h_ref[...], w_ref[:, 0:i0], preferred_element_type=F32)
    not_first = (j > 0).astype(F32)
    not_last = (j < n_blocks - 1).astype(F32)
    row = lax.broadcasted_iota(jnp.int32, (t, i0), 0)
    for bb in range(nbat):
        z = zall[bb * t:(bb + 1) * t]
        r_before = rows + bb * halo + halo - 1
        r_after = rows + nbat * halo + bb * halo
        z_before = zall[r_before:r_before + 1] * not_first
        z_after = zall[r_after:r_after + 1] * not_last
        zm1 = jnp.where(row == 0, z_before, pltpu.roll(z, 1, axis=0))
        zp1 = jnp.where(row == t - 1, z_after, pltpu.roll(z, t - 1, axis=0))
        u = zm1 * cw_ref[0:1, :] + z * cw_ref[1:2, :] + zp1 * cw_ref[2:3, :] + cb_ref[...]
        x0 = u[:, 0:D_HY]
        s = u[:, D_HY:2 * D_HY] * u[:, 2 * D_HY:]
        x0_ref[bb] = x0.astype(BF16)
        p_ref[bb] = (x0 * (s * skip_ref[...])).astype(BF16)
        st_ref[:, bb * t:(bb + 1) * t] = s.T.astype(BF16)

    hm = h_ref[0:rows, :]
    bd = bd_ref[...]
    za = jnp.dot(hm, w_ref[:, i0:i1], preferred_element_type=F32)
    per_edge = MXU_EDGE // LANES
    for c in range(2 * D_ATT // MXU_EDGE):
        sl = slice(c * MXU_EDGE, (c + 1) * MXU_EDGE)
        z = za[:, sl]
        zn = z * lax.rsqrt(_head_mean_square(z, bd) + RMS_EPS) * qkg_ref[:, sl]
        for k in range(per_edge):
            stage_ref[c * per_edge + k] = zn[:, k * LANES:(k + 1) * LANES]
    for c in range(2 * D_ATT // LANES, 3 * D_ATT // LANES):
        stage_ref[c] = za[:, c * LANES:(c + 1) * LANES]
    for c in range(3 * D_ATT // LANES):
        cols = slice(c * LANES, (c + 1) * LANES)
        for bb in range(nbat):
            q1_ref[bb, :, cols] = stage_ref[c, bb * t:(bb + 1) * t, :].astype(BF16)
            for dil, ref in ((4, q4_ref), (16, q16_ref)):
                for r in range(dil):
                    rows_r = pl.ds(bb * t + r, t // dil, stride=dil)
                    ref[bb, r, :, cols] = stage_ref[c, rows_r, :].astype(BF16)

    zq = jnp.dot(hm, w_ref[:, i1:i2], preferred_element_type=F32)
    zq = zq * lax.rsqrt(_head_mean_square(zq, bd) + RMS_EPS) * mqg_ref[...]
    qm_ref[...] = zq.astype(BF16).reshape(nbat, t, D_MEM)
    gate = jnp.dot(hm, w_ref[:, i2:], preferred_element_type=F32)
    gs_ref[...] = (gate * jax.nn.sigmoid(gate)).astype(BF16).reshape(nbat, t, D_MIX)


def _in_proj(x, g, w, bd, qkg, mqg, cw, cb, skip):
    batch, seq, d = x.shape
    d_in = w.shape[1]
    t = SEQ_BLOCK
    nb = seq // t
    nbat = TILE_BATCH
    halo = SUBLANES
    per = t // halo
    const = lambda b, j: (0, 0)
    tok = lambda width: pl.BlockSpec((nbat, t, width), lambda b, j: (b, j, 0))
    res = lambda dil: pl.BlockSpec((nbat, dil, t // dil, 3 * D_ATT), lambda b, j: (b, 0, j, 0))
    return pl.pallas_call(
        functools.partial(_in_proj_kernel, n_blocks=nb),
        grid=(batch // nbat, nb),
        in_specs=[tok(d),
                  pl.BlockSpec((nbat, halo, d), lambda b, j: (b, jnp.maximum(j * per - 1, 0), 0)),
                  pl.BlockSpec((nbat, halo, d),
                               lambda b, j: (b, jnp.minimum((j + 1) * per, seq // halo - 1), 0)),
                  pl.BlockSpec((1, d), const),
                  pl.BlockSpec((d, d_in), const),
                  pl.BlockSpec((MXU_EDGE, MXU_EDGE), const),
                  pl.BlockSpec((1, 2 * D_ATT), const),
                  pl.BlockSpec((1, D_MEM), const),
                  pl.BlockSpec((3, 3 * D_HY), const),
                  pl.BlockSpec((1, 3 * D_HY), const),
                  pl.BlockSpec((1, D_HY), const)],
        out_specs=[pl.BlockSpec((D_HY, nbat * t), lambda b, j: (0, j * (batch // nbat) + b)),
                   tok(D_HY), tok(D_HY), tok(3 * D_ATT), res(4), res(16), tok(D_MEM), tok(D_MIX)],
        out_shape=[jax.ShapeDtypeStruct((D_HY, nb * batch * t), BF16),
                   jax.ShapeDtypeStruct((batch, seq, D_HY), BF16),
                   jax.ShapeDtypeStruct((batch, seq, D_HY), BF16),
                   jax.ShapeDtypeStruct((batch, seq, 3 * D_ATT), BF16),
                   jax.ShapeDtypeStruct((batch, 4, seq // 4, 3 * D_ATT), BF16),
                   jax.ShapeDtypeStruct((batch, 16, seq // 16, 3 * D_ATT), BF16),
                   jax.ShapeDtypeStruct((batch, seq, D_MEM), BF16),
                   jax.ShapeDtypeStruct((batch, seq, D_MIX), BF16)],
        scratch_shapes=[pltpu.VMEM((nbat * (t + 2 * halo), d), BF16),
                        pltpu.VMEM((3 * D_ATT // LANES, nbat * t, LANES), F32)],
        compiler_params=_params(2),
        name="in_proj",
    )(x, x, x, g, w, bd, qkg, mqg, cw, cb, skip)


def _hyena_filter_kernel(band_ref, w1t_ref, b1_ref, fr_ref, w2t_ref, b2_ref, w3f_ref, w3b_ref,
                         dl_ref, g_ref, nrm_ref, ss_ref, *, seq, n_tiles):
    i = pl.program_id(0)
    tl = g_ref.shape[1]
    hi = lax.Precision.HIGHEST
    m = i * tl + lax.broadcasted_iota(jnp.int32, (1, tl), 1)
    lag = m - seq
    pos = jnp.abs(lag).astype(F32)
    tt = pos / float(max(seq - 1, 1))
    ang = ((2.0 * math.pi / seq) * pos) * band_ref[...]
    frow = lax.broadcasted_iota(jnp.int32, (FEAT_ROWS, tl), 0)
    feats = jnp.where(frow == 0, tt,
                      jnp.where(frow <= FILTER_BANDS, jnp.cos(ang),
                                jnp.where(frow <= 2 * FILTER_BANDS, -jnp.sin(ang), 0.0)))
    fr = fr_ref[...]
    h = jnp.sin(fr * (jnp.dot(w1t_ref[...], feats, precision=hi, preferred_element_type=F32)
                      + b1_ref[...]))
    h = jnp.sin(fr * (jnp.dot(w2t_ref[...], h, precision=hi, preferred_element_type=F32)
                      + b2_ref[...]))
    fwd = jnp.dot(w3f_ref[...], h, precision=hi, preferred_element_type=F32)
    bwd = jnp.dot(w3b_ref[...], h, precision=hi, preferred_element_type=F32)
    decay = jnp.exp(-tt * dl_ref[...])
    val = jnp.where(lag > 0, fwd, jnp.where(lag < 0, bwd, fwd + bwd)) * decay
    val = jnp.where(m == 0, 0.0, val)
    g_ref[...] = val

    @pl.when(i == 0)
    def _():
        ss_ref[...] = jnp.zeros_like(ss_ref)

    ss_ref[...] += jnp.sum(val * val, axis=1, keepdims=True)

    @pl.when(i == n_tiles - 1)
    def _():
        nrm_ref[...] = lax.rsqrt(ss_ref[...] + 1e-12)


def _hyena_filter(seq, band, w1t, b1, fr, w2t, b2, w3f, w3b, deltas):
    tl = min(FILTER_TILE, 2 * seq)
    n_tiles = 2 * seq // tl
    const = lambda i: (0, 0)
    full = lambda a: pl.BlockSpec(a.shape, const)
    args = (band, w1t, b1, fr, w2t, b2, w3f, w3b, deltas)
    return pl.pallas_call(
        functools.partial(_hyena_filter_kernel, seq=seq, n_tiles=n_tiles),
        grid=(n_tiles,),
        in_specs=[full(a) for a in args],
        out_specs=[pl.BlockSpec((D_HY, tl), lambda i: (0, i)),
                   pl.BlockSpec((D_HY, 1), const)],
        out_shape=[jax.ShapeDtypeStruct((D_HY, 2 * seq), F32),
                   jax.ShapeDtypeStruct((D_HY, 1), F32)],
        scratch_shapes=[pltpu.VMEM((D_HY, 1), F32)],
        compiler_params=_params(1),
        name="hyena_filter",
    )(*args)


def _conv_one_channel(a_ref, g_ref, o_ref, e_ref, acc_ref, batch, nb):
    t = SEQ_BLOCK
    seq = nb * t
    width = 2 * seq
    half = t // 2
    chunk = min(ROLL_CHUNK, width)
    for c in range(width // chunk):
        m0 = c * chunk
        lo = max(m0 - half, 0)
        win = g_ref[:, lo:m0 + chunk]
        rolled = pltpu.roll(jnp.broadcast_to(win, (half, win.shape[1])), 0, axis=1,
                            stride=1, stride_axis=0)
        blk = rolled[:, m0 - lo:].astype(BF16)
        e_ref[0:half, m0:m0 + chunk] = blk
        hi = min(m0 + half + chunk, width)
        e_ref[half:t, m0 + half:hi] = blk[:, 0:hi - m0 - half]
    acc_ref[...] = jnp.zeros_like(acc_ref)
    for d in range(-(nb - 1), nb):
        base = seq + t * d
        if d >= 0:
            j0 = 0
            nj = nb - d
            nj += nj % 2
        else:
            j0 = -d - ((-d) % 2)
            nj = nb - j0
        rows = batch * nj
        r_in = batch * j0
        r_out = batch + batch * (j0 + d)
        acc_ref[r_out:r_out + rows, :] += jnp.dot(
            a_ref[r_in:r_in + rows, :], e_ref[:, base:base + t], preferred_element_type=F32)
    o_ref[...] = acc_ref[batch:batch + batch * nb, :]


def _hyena_conv_kernel(a_ref, g_ref, o_ref, e_ref, acc_ref, *, batch, nb):
    def body(ci, carry):
        _conv_one_channel(a_ref.at[ci], g_ref.at[ci], o_ref.at[ci], e_ref, acc_ref, batch, nb)
        return carry

    lax.fori_loop(0, a_ref.shape[0], body, 0)


def _hyena_conv(st3, g3, batch, nb):
    t = SEQ_BLOCK
    rows = nb * batch
    cg = CONV_CHANNELS
    grp = lambda c: (c, 0, 0)
    return pl.pallas_call(
        functools.partial(_hyena_conv_kernel, batch=batch, nb=nb),
        grid=(D_HY // cg,),
        in_specs=[pl.BlockSpec((cg, rows, t), grp),
                  pl.BlockSpec((cg, 1, 2 * nb * t), grp)],
        out_specs=pl.BlockSpec((cg, rows, t), grp),
        out_shape=jax.ShapeDtypeStruct((D_HY, rows, t), F32),
        scratch_shapes=[pltpu.VMEM((t, 2 * nb * t), BF16),
                        pltpu.VMEM((rows + 2 * batch, t), F32)],
        compiler_params=_params(1),
        name="hyena_conv",
    )(st3, g3)


def _dil_attn_kernel(bucket_ref, relb_ref, q_ref, k_ref, v_ref, kp_ref, vp_ref, kn_ref, vn_ref,
                     o_ref, lse_ref, kall_ref, vall_ref, bias_ref, *, n):
    tq = q_ref.shape[0]
    side = ATT_SIDE
    first = (pl.program_id(0) == 0) & (pl.program_id(1) == 0) & (pl.program_id(2) == 0)

    @pl.when(first)
    def _():
        bucket = bucket_ref[...]
        for h in range(N_ATT_HEADS):
            bias_ref[h] = jnp.full(bucket.shape, NEG_INF, F32)

        def body(bk, carry):
            hit = bucket == bk
            for h in range(N_ATT_HEADS):
                bias_ref[h] = jnp.where(hit, relb_ref[bk * N_ATT_HEADS + h], bias_ref[h])
            return carry

        lax.fori_loop(0, N_BUCKETS, body, 0)

    kall_ref[0:side, :] = kp_ref[...]
    kall_ref[side:side + tq, :] = k_ref[...]
    kall_ref[side + tq:, :] = kn_ref[...]
    vall_ref[0:side, :] = vp_ref[...]
    vall_ref[side:side + tq, :] = v_ref[...]
    vall_ref[side + tq:, :] = vn_ref[...]

    i = pl.program_id(2)
    lane = lax.broadcasted_iota(jnp.int32, (Q_BLOCK, LANES), 1)
    low_head = lane < HEAD_DIM
    head_sel = [low_head[0:1, :].astype(BF16), jnp.logical_not(low_head)[0:1, :].astype(BF16)]
    for j in range(tq // Q_BLOCK):
        r0 = j * Q_BLOCK
        kpos = i * tq + r0 - side + lax.broadcasted_iota(jnp.int32, (1, 2 * Q_BLOCK), 1)
        in_seq = (kpos >= 0) & (kpos < n)
        for hp in range(N_ATT_HEADS // 2):
            cols = slice(hp * LANES, (hp + 1) * LANES)
            q2 = q_ref[r0:r0 + Q_BLOCK, cols]
            k2 = kall_ref[r0:r0 + 2 * Q_BLOCK, cols]
            v2 = vall_ref[r0:r0 + 2 * Q_BLOCK, cols]
            outs = []
            lses = []
            for hh in range(2):
                qh = q2 * head_sel[hh]
                s = lax.dot_general(qh, k2, (((1,), (1,)), ((), ())),
                                    preferred_element_type=F32)
                s = jnp.where(in_seq, s + bias_ref[2 * hp + hh], NEG_INF)
                mx = jnp.max(s, axis=-1, keepdims=True)
                p = jnp.exp(s - mx)
                den = jnp.sum(p, axis=-1, keepdims=True)
                o = jnp.dot(p.astype(BF16), v2, preferred_element_type=F32)
                outs.append(o * (1.0 / den))
                lses.append(jnp.broadcast_to(mx + jnp.log(den), o.shape))
            o_ref[r0:r0 + Q_BLOCK, cols] = jnp.where(low_head, outs[0], outs[1])
            lse_ref[r0:r0 + Q_BLOCK, cols] = jnp.where(low_head, lses[0], lses[1])


def _attn_bucket_table(dil):
    half = N_BUCKETS // 2
    max_exact = half // 2
    qi = np.arange(Q_BLOCK)[:, None]
    kj = np.arange(2 * Q_BLOCK)[None, :]
    rel_steps = kj - ATT_SIDE - qi
    rel = rel_steps * dil
    ret = np.where(rel > 0, half, 0)
    na = np.abs(rel)
    large = max_exact + (np.log(np.maximum(na, 1).astype(np.float32) / np.float32(max_exact))
                         / np.float32(math.log(MAX_DISTANCE / max_exact))
                         * np.float32(half - max_exact)).astype(np.int32)
    large = np.minimum(large, half - 1)
    bucket = ret + np.where(na < max_exact, na, large)
    return np.where(np.abs(rel_steps) <= ATT_SIDE, bucket, -1).astype(np.int32)


def _dil_attn(qkv, relb_flat, dil):
    batch, _, n, _ = qkv.shape
    tq = min(n, 512)
    nt = n // tq
    side = ATT_SIDE
    per = tq // side
    bucket = jnp.asarray(_attn_bucket_table(dil))
    main = lambda c: (lambda b, r, i: (b, r, i, c))
    prev = lambda c: (lambda b, r, i: (b, r, jnp.maximum(i * per - 1, 0), c))
    nxt = lambda c: (lambda b, r, i: (b, r, jnp.minimum((i + 1) * per, n // side - 1), c))
    blk = lambda rows, imap: pl.BlockSpec((None, None, rows, D_ATT), imap)
    return pl.pallas_call(
        functools.partial(_dil_attn_kernel, n=n),
        grid=(batch, dil, nt),
        in_specs=[pl.BlockSpec(bucket.shape, lambda b, r, i: (0, 0)),
                  pl.BlockSpec(memory_space=pltpu.SMEM),
                  blk(tq, main(0)), blk(tq, main(1)), blk(tq, main(2)),
                  blk(side, prev(1)), blk(side, prev(2)),
                  blk(side, nxt(1)), blk(side, nxt(2))],
        out_specs=[blk(tq, main(0)), blk(tq, main(0))],
        out_shape=[jax.ShapeDtypeStruct((batch, dil, n, D_ATT), F32),
                   jax.ShapeDtypeStruct((batch, dil, n, D_ATT), F32)],
        scratch_shapes=[pltpu.VMEM((tq + 2 * side, D_ATT), BF16),
                        pltpu.VMEM((tq + 2 * side, D_ATT), BF16),
                        pltpu.VMEM((N_ATT_HEADS, Q_BLOCK, 2 * Q_BLOCK), F32)],
        compiler_params=_params(3),
        name=f"dil_attn_d{dil}",
    )(bucket, relb_flat, qkv, qkv, qkv, qkv, qkv, qkv, qkv)


def _mem_kv_kernel(m_ref, g_ref, w_ref, bd_ref, kg_ref, kv_ref):
    x = m_ref[...]
    ms = jnp.mean(x * x, axis=-1, keepdims=True)
    h = (x * lax.rsqrt(ms + RMS_EPS) * g_ref[...]).astype(BF16)
    kv = jnp.dot(h, w_ref[...], preferred_element_type=F32)
    k = kv[:, 0:D_MEM]
    kn = k * lax.rsqrt(_head_mean_square(k, bd_ref[...]) + RMS_EPS) * kg_ref[...]
    kv_ref[:, 0:D_MEM] = kn.astype(BF16)
    kv_ref[:, D_MEM:] = kv[:, D_MEM:].astype(BF16)


def _mem_kv(mem, g, w, bd, kg):
    batch, n_mem, d = mem.shape
    const = lambda b: (0, 0)
    return pl.pallas_call(
        _mem_kv_kernel,
        grid=(batch,),
        in_specs=[pl.BlockSpec((None, n_mem, d), lambda b: (b, 0, 0)),
                  pl.BlockSpec((1, d), const),
                  pl.BlockSpec((d, 2 * D_MEM), const),
                  pl.BlockSpec((MXU_EDGE, MXU_EDGE), const),
                  pl.BlockSpec((1, D_MEM), const)],
        out_specs=pl.BlockSpec((None, n_mem, 2 * D_MEM), lambda b: (b, 0, 0)),
        out_shape=jax.ShapeDtypeStruct((batch, n_mem, 2 * D_MEM), BF16),
        compiler_params=_params(1),
        name="mem_kv",
    )(mem, g, w, bd, kg)


def _out_proj_kernel(x_ref, ct_ref, nrm_ref, x0_ref, p_ref, o1_ref, l1_ref, o4_ref, l4_ref,
                     o16_ref, l16_ref, qm_ref, kv_ref, gs_ref, w_ref, y_ref,
                     so4_ref, sl4_ref, so16_ref, sl16_ref):
    nbat, t, d = x_ref.shape
    rows = nbat * t
    conv = (ct_ref[...] * nrm_ref[...]).T
    y_hy = (x0_ref[...].astype(F32).reshape(rows, D_HY) * conv
            + p_ref[...].astype(F32).reshape(rows, D_HY))

    att_parts = []
    for c in range(D_ATT // LANES):
        cols = slice(c * LANES, (c + 1) * LANES)
        for bb in range(nbat):
            for dil, src_o, src_l, dst_o, dst_l in ((4, o4_ref, l4_ref, so4_ref, sl4_ref),
                                                    (16, o16_ref, l16_ref, so16_ref, sl16_ref)):
                for r in range(dil):
                    rows_r = pl.ds(bb * t + r, t // dil, stride=dil)
                    dst_o[c, rows_r, :] = src_o[bb, r, :, cols]
                    dst_l[c, rows_r, :] = src_l[bb, r, :, cols]
        l1 = l1_ref[:, :, cols].reshape(rows, LANES)
        l2, l3 = sl4_ref[c], sl16_ref[c]
        top = jnp.maximum(jnp.maximum(l1, l2), l3)
        e1, e2, e3 = jnp.exp(l1 - top), jnp.exp(l2 - top), jnp.exp(l3 - top)
        att_parts.append((e1 * o1_ref[:, :, cols].reshape(rows, LANES)
                          + e2 * so4_ref[c] + e3 * so16_ref[c]) / (e1 + e2 + e3))

    lane = lax.broadcasted_iota(jnp.int32, (t, LANES), 1)
    low_head = lane < HEAD_DIM
    head_sel = [low_head[0:1, :].astype(BF16), jnp.logical_not(low_head)[0:1, :].astype(BF16)]
    mem_rows = []
    for bb in range(nbat):
        parts = []
        for hp in range(N_MEM_HEADS // 2):
            cols = slice(hp * LANES, (hp + 1) * LANES)
            q2 = qm_ref[bb, :, cols]
            k2 = kv_ref[bb, :, cols]
            v2 = kv_ref[bb, :, D_MEM + hp * LANES:D_MEM + (hp + 1) * LANES]
            outs = []
            for hh in range(2):
                qh = q2 * head_sel[hh]
                s = lax.dot_general(qh, k2, (((1,), (1,)), ((), ())),
                                    preferred_element_type=F32)
                mx = jnp.max(s, axis=-1, keepdims=True)
                p = jnp.exp(s - mx)
                den = jnp.sum(p, axis=-1, keepdims=True)
                outs.append(jnp.dot(p.astype(BF16), v2, preferred_element_type=F32)
                            * (1.0 / den))
            parts.append(jnp.where(low_head, outs[0], outs[1]))
        mem_rows.append(jnp.concatenate(parts, axis=-1))
    y_mem = jnp.concatenate(mem_rows, axis=0)
    mixed = (jnp.concatenate([y_hy] + att_parts + [y_mem], axis=-1)
             * gs_ref[...].astype(F32).reshape(rows, D_MIX))
    y = x_ref[...].reshape(rows, d) + jnp.dot(mixed.astype(BF16), w_ref[...],
                                              preferred_element_type=F32)
    y_ref[...] = y.reshape(nbat, t, d)


def _out_proj(x, conv_t, nrm, x0, p, att, qm, kv, gs, w):
    batch, seq, d = x.shape
    t = SEQ_BLOCK
    nb = seq // t
    nbat = TILE_BATCH
    const = lambda b, j: (0, 0)
    tok = lambda width: pl.BlockSpec((nbat, t, width), lambda b, j: (b, j, 0))
    res = lambda dil: pl.BlockSpec((nbat, dil, t // dil, D_ATT), lambda b, j: (b, 0, j, 0))
    nat = pl.BlockSpec((nbat, None, t, D_ATT), lambda b, j: (b, 0, j, 0))
    (o1, l1), (o4, l4), (o16, l16) = att
    return pl.pallas_call(
        _out_proj_kernel,
        grid=(batch // nbat, nb),
        in_specs=[tok(d),
                  pl.BlockSpec((D_HY, nbat * t), lambda b, j: (0, j * (batch // nbat) + b)),
                  pl.BlockSpec((D_HY, 1), const),
                  tok(D_HY), tok(D_HY), nat, nat, res(4), res(4), res(16), res(16),
                  tok(D_MEM),
                  pl.BlockSpec((nbat, N_MEM, 2 * D_MEM), lambda b, j: (b, 0, 0)),
                  tok(D_MIX),
                  pl.BlockSpec((D_MIX, d), const)],
        out_specs=tok(d),
        out_shape=jax.ShapeDtypeStruct((batch, seq, d), F32),
        scratch_shapes=[pltpu.VMEM((D_ATT // LANES, nbat * t, LANES), F32)] * 4,
        compiler_params=_params(2),
        name="out_proj",
    )(x, conv_t, nrm, x0, p, o1, l1, o4, l4, o16, l16, qm, kv, gs, w)


def _encoder_layer(x, mem, norm_in, w_in, hy_conv_w, hy_conv_b, hy_filt_w1, hy_filt_b1,
                   hy_filt_freq, hy_filt_w2, hy_filt_b2, hy_filt_w3, hy_skip, att_q_norm,
                   att_k_norm, mem_norm, w_mem_kv, mem_q_norm, mem_k_norm, w_out, rel_bias):
    batch, seq, d = x.shape
    assert seq % (SEQ_BLOCK * 2) == 0 and batch % TILE_BATCH == 0 and batch % SUBLANES == 0
    assert all(seq // dil >= Q_BLOCK for _, dil in DILATED_CONFIGS)
    assert all(win // (2 * dil) == ATT_SIDE for win, dil in DILATED_CONFIGS)
    assert tuple(dil for _, dil in DILATED_CONFIGS) == (1, 4, 16)
    nb = seq // SEQ_BLOCK
    scale = HEAD_DIM ** -0.5

    blk = np.arange(MXU_EDGE) // HEAD_DIM
    bd = jnp.asarray((blk[:, None] == blk[None, :]).astype(np.float32) / HEAD_DIM, BF16)
    qkg = jnp.concatenate([jnp.tile(att_q_norm.astype(F32), N_ATT_HEADS) * scale,
                           jnp.tile(att_k_norm.astype(F32), N_ATT_HEADS)])[None, :]
    mqg = (jnp.tile(mem_q_norm.astype(F32), N_MEM_HEADS) * scale)[None, :]
    mkg = jnp.tile(mem_k_norm.astype(F32), N_MEM_HEADS)[None, :]

    st, x0, p, q1, q4, q16, qm, gs = _in_proj(
        x, norm_in[None, :], w_in.astype(BF16), bd, qkg, mqg, hy_conv_w.astype(F32),
        hy_conv_b.astype(F32)[None, :], hy_skip.astype(F32)[None, :])

    band = jnp.linspace(1e-4, FILTER_BANDS - 1, FILTER_BANDS, dtype=F32)
    band_col = jnp.concatenate([jnp.zeros((1,), F32), band, band,
                                jnp.zeros((FEAT_ROWS - 1 - 2 * FILTER_BANDS,), F32)])[:, None]
    w1t = jnp.pad(hy_filt_w1.astype(F32).T, ((0, 0), (0, FEAT_ROWS - hy_filt_w1.shape[0])))
    deltas = jnp.abs(jnp.linspace(math.log(DECAY_TARGET) / SLOW_DECAY_PCT,
                                  math.log(DECAY_TARGET) / FAST_DECAY_PCT, D_HY, dtype=F32))
    w3t = hy_filt_w3.astype(F32).T
    g, nrm = _hyena_filter(seq, band_col, w1t, hy_filt_b1.astype(F32)[:, None],
                           hy_filt_freq.astype(F32)[:, None], hy_filt_w2.astype(F32).T,
                           hy_filt_b2.astype(F32)[:, None], w3t[0:D_HY], w3t[D_HY:],
                           deltas[:, None])

    conv_t = _hyena_conv(st.reshape(D_HY, nb * batch, SEQ_BLOCK),
                         g.reshape(D_HY, 1, 2 * seq), batch, nb)
    conv_t = conv_t.reshape(D_HY, nb * batch * SEQ_BLOCK)

    relb_flat = rel_bias.astype(F32).reshape(-1)
    att = [_dil_attn(q1.reshape(batch, 1, seq, 3 * D_ATT), relb_flat, 1),
           _dil_attn(q4, relb_flat, 4),
           _dil_attn(q16, relb_flat, 16)]

    kv = _mem_kv(mem, mem_norm[None, :], w_mem_kv.astype(BF16), bd, mkg)

    return _out_proj(x, conv_t, nrm, x0, p, att, qm, kv, gs, w_out.astype(BF16))


def kernel(x_prompt, x_sample, mem_prompt, mem_sample, norm_in, w_in, hy_conv_w, hy_conv_b, hy_filt_w1, hy_filt_b1, hy_filt_freq, hy_filt_w2, hy_filt_b2, hy_filt_w3, hy_skip, att_q_norm, att_k_norm, mem_norm, w_mem_kv, mem_q_norm, mem_k_norm, w_out, rel_bias):
    def trunk(x, mem):
        for l in range(norm_in.shape[0]):
            x = _encoder_layer(x, mem, norm_in[l], w_in[l], hy_conv_w[l], hy_conv_b[l],
                               hy_filt_w1[l], hy_filt_b1[l], hy_filt_freq[l], hy_filt_w2[l],
                               hy_filt_b2[l], hy_filt_w3[l], hy_skip[l], att_q_norm[l],
                               att_k_norm[l], mem_norm[l], w_mem_kv[l], mem_q_norm[l],
                               mem_k_norm[l], w_out[l], rel_bias)
        return x

    return (trunk(x_prompt, mem_prompt), trunk(x_sample, mem_sample))
```

```python
import functools
import math

import numpy as np
import jax
import jax.numpy as jnp
from jax import lax
from jax.experimental import pallas as pl
from jax.experimental.pallas import tpu as pltpu

F32 = jnp.float32
BF16 = jnp.bfloat16

N_MEM = 256
HEAD_DIM = 64
D_HY = 384
N_ATT_HEADS = 6
D_ATT = N_ATT_HEADS * HEAD_DIM
N_MEM_HEADS = 4
D_MEM = N_MEM_HEADS * HEAD_DIM
D_MIX = D_HY + D_ATT + D_MEM
FILTER_BANDS = 16
FILTER_HIDDEN = 64
DECAY_TARGET = 1e-2
FAST_DECAY_PCT = 0.3
SLOW_DECAY_PCT = 1.5
DILATED_CONFIGS = ((128, 1), (512, 4), (2048, 16))
N_BUCKETS = 32
MAX_DISTANCE = 1024
RMS_EPS = 1e-6
NEG_INF = -1e30

SUBLANES = 8
LANES = 128
MXU_EDGE = 256

SEQ_BLOCK = MXU_EDGE
TILE_BATCH = 2
Q_BLOCK = 128
ATT_SIDE = 64
ATT_ROWS = 512
FILTER_TILE = 2048
ROLL_CHUNK = 1024
CONV_CHANNELS = 16
VMEM_LIMIT = 56 * 1024 * 1024


def _params(n_axes):
    return pltpu.CompilerParams(dimension_semantics=("arbitrary",) * n_axes,
                                vmem_limit_bytes=VMEM_LIMIT)


def _head_mean_square(z, bd):
    return jnp.dot((z * z).astype(BF16), bd, preferred_element_type=F32)


def _in_proj_kernel(x_ref, xp_ref, xn_ref, g_ref, w_ref, bd_ref, qkg_ref, mqg_ref,
                    cw_ref, cb_ref, skip_ref,
                    st_ref, x0_ref, p_ref, q1_ref, q4_ref, q16_ref, qm_ref, gs_ref,
                    h_ref, stage_ref, res4_ref, *, n_blocks):
    i0 = 3 * D_HY
    i1 = i0 + 3 * D_ATT
    i2 = i1 + D_MEM
    j = pl.program_id(1)
    nbat, t, d = x_ref.shape
    halo = xp_ref.shape[1]
    rows = nbat * t

    def normed(xv):
        ms = jnp.mean(xv * xv, axis=-1, keepdims=True)
        return (xv * lax.rsqrt(ms + RMS_EPS) * g_ref[...]).astype(BF16)

    h_ref[0:rows, :] = normed(x_ref[...].reshape(rows, d))
    h_ref[rows:rows + nbat * halo, :] = normed(xp_ref[...].reshape(nbat * halo, d))
    h_ref[rows + nbat * halo:, :] = normed(xn_ref[...].reshape(nbat * halo, d))
    hm = h_ref[0:rows, :]
    bd = bd_ref[...]

    zall = jnp.dot(h_ref[...], w_ref[:, 0:i0], preferred_element_type=F32)
    za = jnp.dot(hm, w_ref[:, i0:i1], preferred_element_type=F32)

    not_first = (j > 0).astype(F32)
    not_last = (j < n_blocks - 1).astype(F32)
    row = lax.broadcasted_iota(jnp.int32, (t, i0), 0)
    for bb in range(nbat):
        z = zall[bb * t:(bb + 1) * t]
        r_before = rows + bb * halo + halo - 1
        r_after = rows + nbat * halo + bb * halo
        z_before = zall[r_before:r_before + 1] * not_first
        z_after = zall[r_after:r_after + 1] * not_last
        zm1 = jnp.where(row == 0, z_before, pltpu.roll(z, 1, axis=0))
        zp1 = jnp.where(row == t - 1, z_after, pltpu.roll(z, t - 1, axis=0))
        u = zm1 * cw_ref[0:1, :] + z * cw_ref[1:2, :] + zp1 * cw_ref[2:3, :] + cb_ref[...]
        x0 = u[:, 0:D_HY]
        s = u[:, D_HY:2 * D_HY] * u[:, 2 * D_HY:]
        x0_ref[bb] = x0.astype(BF16)
        p_ref[bb] = (x0 * (s * skip_ref[...])).astype(BF16)
        st_ref[:, bb * t:(bb + 1) * t] = s.T.astype(BF16)

    gate = jnp.dot(hm, w_ref[:, i2:], preferred_element_type=F32)

    per_edge = MXU_EDGE // LANES
    for c in range(2 * D_ATT // MXU_EDGE):
        sl = slice(c * MXU_EDGE, (c + 1) * MXU_EDGE)
        z = za[:, sl]
        zn = z * lax.rsqrt(_head_mean_square(z, bd) + RMS_EPS) * qkg_ref[:, sl]
        for k in range(per_edge):
            stage_ref[c * per_edge + k] = zn[:, k * LANES:(k + 1) * LANES]
    for c in range(2 * D_ATT // LANES, 3 * D_ATT // LANES):
        stage_ref[c] = za[:, c * LANES:(c + 1) * LANES]

    zq = jnp.dot(hm, w_ref[:, i1:i2], preferred_element_type=F32)

    silu = gate * (0.5 * jnp.tanh(0.5 * gate) + 0.5)
    gs_ref[...] = silu.astype(BF16).reshape(nbat, t, D_MIX)

    tq = t // 4
    for c in range(3 * D_ATT // LANES):
        cols = slice(c * LANES, (c + 1) * LANES)
        for bb in range(nbat):
            q1_ref[bb, :, cols] = stage_ref[c, bb * t:(bb + 1) * t, :].astype(BF16)
            tmp = res4_ref.at[c * nbat + bb]
            for r4 in range(4):
                sub = stage_ref[c, pl.ds(bb * t + r4, tq, stride=4), :]
                q4_ref[bb, r4, :, cols] = sub.astype(BF16)
                tmp[r4] = sub
            for r4 in range(4):
                for a in range(4):
                    sub = tmp[r4, pl.ds(a, tq // 4, stride=4), :]
                    q16_ref[bb, 4 * a + r4, :, cols] = sub.astype(BF16)

    zq = zq * lax.rsqrt(_head_mean_square(zq, bd) + RMS_EPS) * mqg_ref[...]
    qm_ref[...] = zq.astype(BF16).reshape(nbat, t, D_MEM)


def _in_proj(x, g, w, bd, qkg, mqg, cw, cb, skip):
    batch, seq, d = x.shape
    d_in = w.shape[1]
    t = SEQ_BLOCK
    nb = seq // t
    nbat = TILE_BATCH
    halo = SUBLANES
    per = t // halo
    n_chunks = 3 * D_ATT // LANES
    const = lambda b, j: (0, 0)
    tok = lambda width: pl.BlockSpec((nbat, t, width), lambda b, j: (b, j, 0))
    res = lambda dil: pl.BlockSpec((nbat, dil, t // dil, 3 * D_ATT), lambda b, j: (b, 0, j, 0))
    return pl.pallas_call(
        functools.partial(_in_proj_kernel, n_blocks=nb),
        grid=(batch // nbat, nb),
        in_specs=[tok(d),
                  pl.BlockSpec((nbat, halo, d), lambda b, j: (b, jnp.maximum(j * per - 1, 0), 0)),
                  pl.BlockSpec((nbat, halo, d),
                               lambda b, j: (b, jnp.minimum((j + 1) * per, seq // halo - 1), 0)),
                  pl.BlockSpec((1, d), const),
                  pl.BlockSpec((d, d_in), const),
                  pl.BlockSpec((MXU_EDGE, MXU_EDGE), const),
                  pl.BlockSpec((1, 2 * D_ATT), const),
                  pl.BlockSpec((1, D_MEM), const),
                  pl.BlockSpec((3, 3 * D_HY), const),
                  pl.BlockSpec((1, 3 * D_HY), const),
                  pl.BlockSpec((1, D_HY), const)],
        out_specs=[pl.BlockSpec((D_HY, nbat * t), lambda b, j: (0, j * (batch // nbat) + b)),
                   tok(D_HY), tok(D_HY), tok(3 * D_ATT), res(4), res(16), tok(D_MEM), tok(D_MIX)],
        out_shape=[jax.ShapeDtypeStruct((D_HY, nb * batch * t), BF16),
                   jax.ShapeDtypeStruct((batch, seq, D_HY), BF16),
                   jax.ShapeDtypeStruct((batch, seq, D_HY), BF16),
                   jax.ShapeDtypeStruct((batch, seq, 3 * D_ATT), BF16),
                   jax.ShapeDtypeStruct((batch, 4, seq // 4, 3 * D_ATT), BF16),
                   jax.ShapeDtypeStruct((batch, 16, seq // 16, 3 * D_ATT), BF16),
                   jax.ShapeDtypeStruct((batch, seq, D_MEM), BF16),
                   jax.ShapeDtypeStruct((batch, seq, D_MIX), BF16)],
        scratch_shapes=[pltpu.VMEM((nbat * (t + 2 * halo), d), BF16),
                        pltpu.VMEM((n_chunks, nbat * t, LANES), F32),
                        pltpu.VMEM((n_chunks * nbat, 4, t // 4, LANES), F32)],
        compiler_params=_params(2),
        name="in_proj",
    )(x, x, x, g, w, bd, qkg, mqg, cw, cb, skip)


def _hyena_filter_kernel(band_ref, w1t_ref, w1c_ref, w1s_ref, b1_ref, fr_ref, w2t_ref, b2_ref,
                         w3f_ref, w3b_ref, dl_ref, g_ref, nrm_ref, ss_ref, *, seq, n_tiles):
    i = pl.program_id(0)
    tl = g_ref.shape[1]
    hi = lax.Precision.HIGHEST
    m = i * tl + lax.broadcasted_iota(jnp.int32, (1, tl), 1)
    lag = m - seq
    pos = jnp.abs(lag).astype(F32)
    tt = pos / float(max(seq - 1, 1))
    ang = ((2.0 * math.pi / seq) * pos) * band_ref[...]
    pre = (w1t_ref[...] * tt
           + jnp.dot(w1c_ref[...], jnp.cos(ang), precision=hi, preferred_element_type=F32)
           + jnp.dot(w1s_ref[...], -jnp.sin(ang), precision=hi, preferred_element_type=F32))
    fr = fr_ref[...]
    h = jnp.sin(fr * (pre + b1_ref[...]))
    h = jnp.sin(fr * (jnp.dot(w2t_ref[...], h, precision=hi, preferred_element_type=F32)
                      + b2_ref[...]))
    fwd = jnp.dot(w3f_ref[...], h, precision=hi, preferred_element_type=F32)
    bwd = jnp.dot(w3b_ref[...], h, precision=hi, preferred_element_type=F32)
    decay = jnp.exp(-tt * dl_ref[...])
    val = jnp.where(lag > 0, fwd, jnp.where(lag < 0, bwd, fwd + bwd)) * decay
    val = jnp.where(m == 0, 0.0, val)
    g_ref[...] = val

    @pl.when(i == 0)
    def _():
        ss_ref[...] = jnp.zeros_like(ss_ref)

    ss_ref[...] += jnp.sum(val * val, axis=1, keepdims=True)

    @pl.when(i == n_tiles - 1)
    def _():
        nrm_ref[...] = lax.rsqrt(ss_ref[...] + 1e-12)


def _hyena_filter(seq, band, w1, b1, fr, w2t, b2, w3f, w3b, deltas):
    tl = min(FILTER_TILE, 2 * seq)
    n_tiles = 2 * seq // tl
    const = lambda i: (0, 0)
    full = lambda a: pl.BlockSpec(a.shape, const)
    w1t = w1.T
    args = (band, w1t[:, 0:1], w1t[:, 1:1 + FILTER_BANDS], w1t[:, 1 + FILTER_BANDS:], b1, fr,
            w2t, b2, w3f, w3b, deltas)
    return pl.pallas_call(
        functools.partial(_hyena_filter_kernel, seq=seq, n_tiles=n_tiles),
        grid=(n_tiles,),
        in_specs=[full(a) for a in args],
        out_specs=[pl.BlockSpec((D_HY, tl), lambda i: (0, i)),
                   pl.BlockSpec((D_HY, 1), const)],
        out_shape=[jax.ShapeDtypeStruct((D_HY, 2 * seq), F32),
                   jax.ShapeDtypeStruct((D_HY, 1), F32)],
        scratch_shapes=[pltpu.VMEM((D_HY, 1), F32)],
        compiler_params=_params(1),
        name="hyena_filter",
    )(*args)


def _expand_filter(g_ref, e_ref, nb):
    t = SEQ_BLOCK
    width = 2 * nb * t
    half = t // 2
    chunk = min(ROLL_CHUNK, width)
    for c in range(width // chunk):
        m0 = c * chunk
        lo = max(m0 - half, 0)
        win = g_ref[:, lo:m0 + chunk]
        rolled = pltpu.roll(jnp.broadcast_to(win, (half, win.shape[1])), 0, axis=1,
                            stride=1, stride_axis=0)
        blk = rolled[:, m0 - lo:].astype(BF16)
        e_ref[0:half, m0:m0 + chunk] = blk
        hi = min(m0 + half + chunk, width)
        e_ref[half:t, m0 + half:hi] = blk[:, 0:hi - m0 - half]


def _toeplitz_matmuls(a_ref, e_ref, o_ref, acc_ref, batch, nb):
    t = SEQ_BLOCK
    seq = nb * t
    acc_ref[...] = jnp.zeros_like(acc_ref)
    for d in range(-(nb - 1), nb):
        base = seq + t * d
        if d >= 0:
            j0 = 0
            nj = nb - d
            nj += nj % 2
        else:
            j0 = -d - ((-d) % 2)
            nj = nb - j0
        rows = batch * nj
        r_in = batch * j0
        r_out = batch + batch * (j0 + d)
        acc_ref[r_out:r_out + rows, :] += jnp.dot(
            a_ref[r_in:r_in + rows, :], e_ref[:, base:base + t], preferred_element_type=F32)
    o_ref[...] = acc_ref[batch:batch + batch * nb, :]


def _hyena_conv_kernel(a_ref, g_ref, o_ref, e0_ref, e1_ref, acc0_ref, acc1_ref, *, batch, nb):
    cg = a_ref.shape[0]
    _expand_filter(g_ref.at[0], e0_ref, nb)

    def body(k, carry):
        c0 = 2 * k
        _expand_filter(g_ref.at[c0 + 1], e1_ref, nb)
        _toeplitz_matmuls(a_ref.at[c0], e0_ref, o_ref.at[c0], acc0_ref, batch, nb)
        _expand_filter(g_ref.at[jnp.minimum(c0 + 2, cg - 1)], e0_ref, nb)
        _toeplitz_matmuls(a_ref.at[c0 + 1], e1_ref, o_ref.at[c0 + 1], acc1_ref, batch, nb)
        return carry

    lax.fori_loop(0, cg // 2, body, 0)


def _hyena_conv(st3, g3, batch, nb):
    t = SEQ_BLOCK
    rows = nb * batch
    cg = CONV_CHANNELS
    grp = lambda c: (c, 0, 0)
    expand = pltpu.VMEM((t, 2 * nb * t), BF16)
    acc = pltpu.VMEM((rows + 2 * batch, t), F32)
    return pl.pallas_call(
        functools.partial(_hyena_conv_kernel, batch=batch, nb=nb),
        grid=(D_HY // cg,),
        in_specs=[pl.BlockSpec((cg, rows, t), grp),
                  pl.BlockSpec((cg, 1, 2 * nb * t), grp)],
        out_specs=pl.BlockSpec((cg, rows, t), grp),
        out_shape=jax.ShapeDtypeStruct((D_HY, rows, t), F32),
        scratch_shapes=[expand, expand, acc, acc],
        compiler_params=_params(1),
        name="hyena_conv",
    )(st3, g3)


def _dil_attn_kernel(bucket_ref, relb_ref, q_ref, k_ref, v_ref, kp_ref, vp_ref, kn_ref, vn_ref,
                     o_ref, lse_ref, kall_ref, vall_ref, bias_ref, *, n):
    n_res, tq, _ = q_ref.shape
    side = ATT_SIDE
    first = (pl.program_id(0) == 0) & (pl.program_id(1) == 0) & (pl.program_id(2) == 0)

    @pl.when(first)
    def _():
        bucket = bucket_ref[...]
        for h in range(N_ATT_HEADS):
            bias_ref[h] = jnp.full(bucket.shape, NEG_INF, F32)

        def body(bk, carry):
            hit = bucket == bk
            for h in range(N_ATT_HEADS):
                bias_ref[h] = jnp.where(hit, relb_ref[bk * N_ATT_HEADS + h], bias_ref[h])
            return carry

        lax.fori_loop(0, N_BUCKETS, body, 0)

    i = pl.program_id(2)
    lane = lax.broadcasted_iota(jnp.int32, (Q_BLOCK, LANES), 1)
    low_head = lane < HEAD_DIM
    head_sel = [low_head[0:1, :].astype(BF16), jnp.logical_not(low_head)[0:1, :].astype(BF16)]
    n_sub = tq // Q_BLOCK
    for rr in range(n_res):
        kall_ref[rr, 0:side, :] = kp_ref[rr]
        kall_ref[rr, side:side + tq, :] = k_ref[rr]
        kall_ref[rr, side + tq:, :] = kn_ref[rr]
        vall_ref[rr, 0:side, :] = vp_ref[rr]
        vall_ref[rr, side:side + tq, :] = v_ref[rr]
        vall_ref[rr, side + tq:, :] = vn_ref[rr]
        for j in range(n_sub):
            r0 = j * Q_BLOCK
            at_edge = j == 0 or j == n_sub - 1
            if at_edge:
                kpos = (i * tq + r0 - side
                        + lax.broadcasted_iota(jnp.int32, (1, 2 * Q_BLOCK), 1))
                in_seq = (kpos >= 0) & (kpos < n)
            for hp in range(N_ATT_HEADS // 2):
                cols = slice(hp * LANES, (hp + 1) * LANES)
                q2 = q_ref[rr, r0:r0 + Q_BLOCK, cols]
                k2 = kall_ref[rr, r0:r0 + 2 * Q_BLOCK, cols]
                v2 = vall_ref[rr, r0:r0 + 2 * Q_BLOCK, cols]
                outs = []
                lses = []
                for hh in range(2):
                    qh = q2 * head_sel[hh]
                    s = lax.dot_general(qh, k2, (((1,), (1,)), ((), ())),
                                        preferred_element_type=F32)
                    s = s + bias_ref[2 * hp + hh]
                    if at_edge:
                        s = jnp.where(in_seq, s, NEG_INF)
                    mx = jnp.max(s, axis=-1, keepdims=True)
                    p = jnp.exp(s - mx)
                    den = jnp.sum(p, axis=-1, keepdims=True)
                    o = jnp.dot(p.astype(BF16), v2, preferred_element_type=F32)
                    outs.append(o * (1.0 / den))
                    lses.append(jnp.broadcast_to(mx + jnp.log(den), o.shape))
                o_ref[rr, r0:r0 + Q_BLOCK, cols] = jnp.where(low_head, outs[0], outs[1])
                lse_ref[rr, r0:r0 + Q_BLOCK, cols] = jnp.where(low_head, lses[0], lses[1])


def _attn_bucket_table(dil):
    half = N_BUCKETS // 2
    max_exact = half // 2
    qi = np.arange(Q_BLOCK)[:, None]
    kj = np.arange(2 * Q_BLOCK)[None, :]
    rel_steps = kj - ATT_SIDE - qi
    rel = rel_steps * dil
    ret = np.where(rel > 0, half, 0)
    na = np.abs(rel)
    large = max_exact + (np.log(np.maximum(na, 1).astype(np.float32) / np.float32(max_exact))
                         / np.float32(math.log(MAX_DISTANCE / max_exact))
                         * np.float32(half - max_exact)).astype(np.int32)
    large = np.minimum(large, half - 1)
    bucket = ret + np.where(na < max_exact, na, large)
    return np.where(np.abs(rel_steps) <= ATT_SIDE, bucket, -1).astype(np.int32)


def _dil_attn(qkv, relb_flat, dil):
    batch, _, n, _ = qkv.shape
    tq = min(n, ATT_ROWS)
    nt = n // tq
    n_res = min(dil, ATT_ROWS // tq)
    side = ATT_SIDE
    per = tq // side
    bucket = jnp.asarray(_attn_bucket_table(dil))
    main = lambda c: (lambda b, r, i: (b, r, i, c))
    prev = lambda c: (lambda b, r, i: (b, r, jnp.maximum(i * per - 1, 0), c))
    nxt = lambda c: (lambda b, r, i: (b, r, jnp.minimum((i + 1) * per, n // side - 1), c))
    blk = lambda rows, imap: pl.BlockSpec((None, n_res, rows, D_ATT), imap)
    return pl.pallas_call(
        functools.partial(_dil_attn_kernel, n=n),
        grid=(batch, dil // n_res, nt),
        in_specs=[pl.BlockSpec(bucket.shape, lambda b, r, i: (0, 0)),
                  pl.BlockSpec(memory_space=pltpu.SMEM),
                  blk(tq, main(0)), blk(tq, main(1)), blk(tq, main(2)),
                  blk(side, prev(1)), blk(side, prev(2)),
                  blk(side, nxt(1)), blk(side, nxt(2))],
        out_specs=[blk(tq, main(0)), blk(tq, main(0))],
        out_shape=[jax.ShapeDtypeStruct((batch, dil, n, D_ATT), F32),
                   jax.ShapeDtypeStruct((batch, dil, n, D_ATT), F32)],
        scratch_shapes=[pltpu.VMEM((n_res, tq + 2 * side, D_ATT), BF16),
                        pltpu.VMEM((n_res, tq + 2 * side, D_ATT), BF16),
                        pltpu.VMEM((N_ATT_HEADS, Q_BLOCK, 2 * Q_BLOCK), F32)],
        compiler_params=_params(3),
        name=f"dil_attn_d{dil}",
    )(bucket, relb_flat, qkv, qkv, qkv, qkv, qkv, qkv, qkv)


def _mem_kv_kernel(m_ref, g_ref, w_ref, bd_ref, kg_ref, kv_ref):
    x = m_ref[...]
    ms = jnp.mean(x * x, axis=-1, keepdims=True)
    h = (x * lax.rsqrt(ms + RMS_EPS) * g_ref[...]).astype(BF16)
    kv = jnp.dot(h, w_ref[...], preferred_element_type=F32)
    k = kv[:, 0:D_MEM]
    kn = k * lax.rsqrt(_head_mean_square(k, bd_ref[...]) + RMS_EPS) * kg_ref[...]
    kv_ref[:, 0:D_MEM] = kn.astype(BF16)
    kv_ref[:, D_MEM:] = kv[:, D_MEM:].astype(BF16)


def _mem_kv(mem, g, w, bd, kg):
    batch, n_mem, d = mem.shape
    const = lambda b: (0, 0)
    return pl.pallas_call(
        _mem_kv_kernel,
        grid=(batch,),
        in_specs=[pl.BlockSpec((None, n_mem, d), lambda b: (b, 0, 0)),
                  pl.BlockSpec((1, d), const),
                  pl.BlockSpec((d, 2 * D_MEM), const),
                  pl.BlockSpec((MXU_EDGE, MXU_EDGE), const),
                  pl.BlockSpec((1, D_MEM), const)],
        out_specs=pl.BlockSpec((None, n_mem, 2 * D_MEM), lambda b: (b, 0, 0)),
        out_shape=jax.ShapeDtypeStruct((batch, n_mem, 2 * D_MEM), BF16),
        compiler_params=_params(1),
        name="mem_kv",
    )(mem, g, w, bd, kg)


def _out_proj_kernel(x_ref, ct_ref, nrm_ref, x0_ref, p_ref, o1_ref, l1_ref, o4_ref, l4_ref,
                     o16_ref, l16_ref, qm_ref, kv_ref, gs_ref, w_ref, y_ref,
                     so4_ref, sl4_ref, so16_ref, sl16_ref, tmp_ref, mixed_ref):
    nbat, t, d = x_ref.shape
    rows = nbat * t

    def gated(value, cols):
        gate = gs_ref[:, :, cols].astype(F32).reshape(rows, cols.stop - cols.start)
        mixed_ref[:, cols] = (value * gate).astype(BF16)

    conv = (ct_ref[...] * nrm_ref[...]).T
    y_hy = (x0_ref[...].astype(F32).reshape(rows, D_HY) * conv
            + p_ref[...].astype(F32).reshape(rows, D_HY))
    gated(y_hy, slice(0, D_HY))

    tq = t // 4
    for c in range(D_ATT // LANES):
        cols = slice(c * LANES, (c + 1) * LANES)
        for bb in range(nbat):
            for k, (src4, src16, dst4, dst16) in enumerate(((o4_ref, o16_ref, so4_ref, so16_ref),
                                                           (l4_ref, l16_ref, sl4_ref, sl16_ref))):
                tmp = tmp_ref.at[(c * nbat + bb) * 2 + k]
                for r4 in range(4):
                    dst4[c, pl.ds(bb * t + r4, tq, stride=4), :] = src4[bb, r4, :, cols]
                    for a in range(4):
                        tmp[r4, pl.ds(a, tq // 4, stride=4), :] = src16[bb, 4 * a + r4, :, cols]
                for r4 in range(4):
                    dst16[c, pl.ds(bb * t + r4, tq, stride=4), :] = tmp[r4]
        l1 = l1_ref[:, :, cols].reshape(rows, LANES)
        l2, l3 = sl4_ref[c], sl16_ref[c]
        top = jnp.maximum(jnp.maximum(l1, l2), l3)
        e1, e2, e3 = jnp.exp(l1 - top), jnp.exp(l2 - top), jnp.exp(l3 - top)
        y_att = ((e1 * o1_ref[:, :, cols].reshape(rows, LANES) + e2 * so4_ref[c]
                  + e3 * so16_ref[c]) * (1.0 / (e1 + e2 + e3)))
        gated(y_att, slice(D_HY + c * LANES, D_HY + (c + 1) * LANES))

    lane = lax.broadcasted_iota(jnp.int32, (t, LANES), 1)
    low_head = lane < HEAD_DIM
    head_sel = [low_head[0:1, :].astype(BF16), jnp.logical_not(low_head)[0:1, :].astype(BF16)]
    for hp in range(N_MEM_HEADS // 2):
        cols = slice(hp * LANES, (hp + 1) * LANES)
        mem_rows = []
        for bb in range(nbat):
            q2 = qm_ref[bb, :, cols]
            k2 = kv_ref[bb, :, cols]
            v2 = kv_ref[bb, :, D_MEM + hp * LANES:D_MEM + (hp + 1) * LANES]
            outs = []
            for hh in range(2):
                qh = q2 * head_sel[hh]
                s = lax.dot_general(qh, k2, (((1,), (1,)), ((), ())),
                                    preferred_element_type=F32)
                mx = jnp.max(s, axis=-1, keepdims=True)
                p = jnp.exp(s - mx)
                den = jnp.sum(p, axis=-1, keepdims=True)
                outs.append(jnp.dot(p.astype(BF16), v2, preferred_element_type=F32)
                            * (1.0 / den))
            mem_rows.append(jnp.where(low_head, outs[0], outs[1]))
        base = D_HY + D_ATT + hp * LANES
        gated(jnp.concatenate(mem_rows, axis=0), slice(base, base + LANES))

    y = x_ref[...].reshape(rows, d) + jnp.dot(mixed_ref[...], w_ref[...],
                                              preferred_element_type=F32)
    y_ref[...] = y.reshape(nbat, t, d)


def _out_proj(x, conv_t, nrm, x0, p, att, qm, kv, gs, w):
    batch, seq, d = x.shape
    t = SEQ_BLOCK
    nb = seq // t
    nbat = TILE_BATCH
    n_chunks = D_ATT // LANES
    const = lambda b, j: (0, 0)
    tok = lambda width: pl.BlockSpec((nbat, t, width), lambda b, j: (b, j, 0))
    res = lambda dil: pl.BlockSpec((nbat, dil, t // dil, D_ATT), lambda b, j: (b, 0, j, 0))
    nat = pl.BlockSpec((nbat, None, t, D_ATT), lambda b, j: (b, 0, j, 0))
    (o1, l1), (o4, l4), (o16, l16) = att
    natural = pltpu.VMEM((n_chunks, nbat * t, LANES), F32)
    return pl.pallas_call(
        _out_proj_kernel,
        grid=(batch // nbat, nb),
        in_specs=[tok(d),
                  pl.BlockSpec((D_HY, nbat * t), lambda b, j: (0, j * (batch // nbat) + b)),
                  pl.BlockSpec((D_HY, 1), const),
                  tok(D_HY), tok(D_HY), nat, nat, res(4), res(4), res(16), res(16),
                  tok(D_MEM),
                  pl.BlockSpec((nbat, N_MEM, 2 * D_MEM), lambda b, j: (b, 0, 0)),
                  tok(D_MIX),
                  pl.BlockSpec((D_MIX, d), const)],
        out_specs=tok(d),
        out_shape=jax.ShapeDtypeStruct((batch, seq, d), F32),
        scratch_shapes=[natural, natural, natural, natural,
                        pltpu.VMEM((n_chunks * nbat * 2, 4, t // 4, LANES), F32),
                        pltpu.VMEM((nbat * t, D_MIX), BF16)],
        compiler_params=_params(2),
        name="out_proj",
    )(x, conv_t, nrm, x0, p, o1, l1, o4, l4, o16, l16, qm, kv, gs, w)


def _encoder_layer(x, mem, norm_in, w_in, hy_conv_w, hy_conv_b, hy_filt_w1, hy_filt_b1,
                   hy_filt_freq, hy_filt_w2, hy_filt_b2, hy_filt_w3, hy_skip, att_q_norm,
                   att_k_norm, mem_norm, w_mem_kv, mem_q_norm, mem_k_norm, w_out, rel_bias):
    batch, seq, d = x.shape
    assert seq % (SEQ_BLOCK * 2) == 0 and batch % TILE_BATCH == 0 and batch % SUBLANES == 0
    assert all(seq // dil >= Q_BLOCK for _, dil in DILATED_CONFIGS)
    assert all(win // (2 * dil) == ATT_SIDE for win, dil in DILATED_CONFIGS)
    assert tuple(dil for _, dil in DILATED_CONFIGS) == (1, 4, 16)
    nb = seq // SEQ_BLOCK
    scale = HEAD_DIM ** -0.5

    blk = np.arange(MXU_EDGE) // HEAD_DIM
    bd = jnp.asarray((blk[:, None] == blk[None, :]).astype(np.float32) / HEAD_DIM, BF16)
    qkg = jnp.concatenate([jnp.tile(att_q_norm.astype(F32), N_ATT_HEADS) * scale,
                           jnp.tile(att_k_norm.astype(F32), N_ATT_HEADS)])[None, :]
    mqg = (jnp.tile(mem_q_norm.astype(F32), N_MEM_HEADS) * scale)[None, :]
    mkg = jnp.tile(mem_k_norm.astype(F32), N_MEM_HEADS)[None, :]

    st, x0, p, q1, q4, q16, qm, gs = _in_proj(
        x, norm_in[None, :], w_in.astype(BF16), bd, qkg, mqg, hy_conv_w.astype(F32),
        hy_conv_b.astype(F32)[None, :], hy_skip.astype(F32)[None, :])

    band = jnp.linspace(1e-4, FILTER_BANDS - 1, FILTER_BANDS, dtype=F32)[:, None]
    deltas = jnp.abs(jnp.linspace(math.log(DECAY_TARGET) / SLOW_DECAY_PCT,
                                  math.log(DECAY_TARGET) / FAST_DECAY_PCT, D_HY, dtype=F32))
    w3t = hy_filt_w3.astype(F32).T
    g, nrm = _hyena_filter(seq, band, hy_filt_w1.astype(F32), hy_filt_b1.astype(F32)[:, None],
                           hy_filt_freq.astype(F32)[:, None], hy_filt_w2.astype(F32).T,
                           hy_filt_b2.astype(F32)[:, None], w3t[0:D_HY], w3t[D_HY:],
                           deltas[:, None])

    conv_t = _hyena_conv(st.reshape(D_HY, nb * batch, SEQ_BLOCK),
                         g.reshape(D_HY, 1, 2 * seq), batch, nb)
    conv_t = conv_t.reshape(D_HY, nb * batch * SEQ_BLOCK)

    relb_flat = rel_bias.astype(F32).reshape(-1)
    att = [_dil_attn(q1.reshape(batch, 1, seq, 3 * D_ATT), relb_flat, 1),
           _dil_attn(q4, relb_flat, 4),
           _dil_attn(q16, relb_flat, 16)]

    kv = _mem_kv(mem, mem_norm[None, :], w_mem_kv.astype(BF16), bd, mkg)

    return _out_proj(x, conv_t, nrm, x0, p, att, qm, kv, gs, w_out.astype(BF16))


def kernel(x_prompt, x_sample, mem_prompt, mem_sample, norm_in, w_in, hy_conv_w, hy_conv_b, hy_filt_w1, hy_filt_b1, hy_filt_freq, hy_filt_w2, hy_filt_b2, hy_filt_w3, hy_skip, att_q_norm, att_k_norm, mem_norm, w_mem_kv, mem_q_norm, mem_k_norm, w_out, rel_bias):
    def trunk(x, mem):
        for l in range(norm_in.shape[0]):
            x = _encoder_layer(x, mem, norm_in[l], w_in[l], hy_conv_w[l], hy_conv_b[l],
                               hy_filt_w1[l], hy_filt_b1[l], hy_filt_freq[l], hy_filt_w2[l],
                               hy_filt_b2[l], hy_filt_w3[l], hy_skip[l], att_q_norm[l],
                               att_k_norm[l], mem_norm[l], w_mem_kv[l], mem_q_norm[l],
                               mem_k_norm[l], w_out[l], rel_bias)
        return x

    return (trunk(x_prompt, mem_prompt), trunk(x_sample, mem_sample))
```

```python
import functools
import math

import numpy as np
import jax
import jax.numpy as jnp
from jax import lax
from jax.experimental import pallas as pl
from jax.experimental.pallas import tpu as pltpu

F32 = jnp.float32
BF16 = jnp.bfloat16

N_MEM = 256
HEAD_DIM = 64
D_HY = 384
N_ATT_HEADS = 6
D_ATT = N_ATT_HEADS * HEAD_DIM
N_MEM_HEADS = 4
D_MEM = N_MEM_HEADS * HEAD_DIM
D_MIX = D_HY + D_ATT + D_MEM
FILTER_BANDS = 16
FILTER_HIDDEN = 64
DECAY_TARGET = 1e-2
FAST_DECAY_PCT = 0.3
SLOW_DECAY_PCT = 1.5
DILATED_CONFIGS = ((128, 1), (512, 4), (2048, 16))
N_BUCKETS = 32
MAX_DISTANCE = 1024
RMS_EPS = 1e-6
NEG_INF = -1e30

SUBLANES = 8
LANES = 128
MXU_EDGE = 256

SEQ_BLOCK = MXU_EDGE
TILE_BATCH = 2
Q_BLOCK = 128
ATT_SIDE = 64
ATT_ROWS = 512
FILTER_TILE = 2048
ROLL_CHUNK = 1024
CONV_CHANNELS = 16
CONV_GROUP = 4
VMEM_LIMIT = 56 * 1024 * 1024


def _params(n_axes):
    return pltpu.CompilerParams(dimension_semantics=("arbitrary",) * n_axes,
                                vmem_limit_bytes=VMEM_LIMIT)


def _head_mean_square(z, bd):
    return jnp.dot((z * z).astype(BF16), bd, preferred_element_type=F32)


def _in_proj_kernel(x_ref, xp_ref, xn_ref, g_ref, w_ref, bd_ref, qkg_ref, mqg_ref,
                    cw_ref, cb_ref, skip_ref,
                    st_ref, x0_ref, p_ref, q1_ref, q4_ref, q16_ref, qm_ref, gs_ref,
                    h_ref, stage_ref, res4_ref, *, n_blocks):
    i0 = 3 * D_HY
    i1 = i0 + 3 * D_ATT
    i2 = i1 + D_MEM
    j = pl.program_id(1)
    nbat, t, d = x_ref.shape
    halo = xp_ref.shape[1]
    rows = nbat * t

    def normed(xv):
        ms = jnp.mean(xv * xv, axis=-1, keepdims=True)
        return (xv * lax.rsqrt(ms + RMS_EPS) * g_ref[...]).astype(BF16)

    h_ref[0:rows, :] = normed(x_ref[...].reshape(rows, d))
    h_ref[rows:rows + nbat * halo, :] = normed(xp_ref[...].reshape(nbat * halo, d))
    h_ref[rows + nbat * halo:, :] = normed(xn_ref[...].reshape(nbat * halo, d))
    hm = h_ref[0:rows, :]
    bd = bd_ref[...]

    zall = jnp.dot(h_ref[...], w_ref[:, 0:i0], preferred_element_type=F32)
    za = jnp.dot(hm, w_ref[:, i0:i1], preferred_element_type=F32)

    not_first = (j > 0).astype(F32)
    not_last = (j < n_blocks - 1).astype(F32)
    row = lax.broadcasted_iota(jnp.int32, (t, i0), 0)
    for bb in range(nbat):
        z = zall[bb * t:(bb + 1) * t]
        r_before = rows + bb * halo + halo - 1
        r_after = rows + nbat * halo + bb * halo
        z_before = zall[r_before:r_before + 1] * not_first
        z_after = zall[r_after:r_after + 1] * not_last
        zm1 = jnp.where(row == 0, z_before, pltpu.roll(z, 1, axis=0))
        zp1 = jnp.where(row == t - 1, z_after, pltpu.roll(z, t - 1, axis=0))
        u = zm1 * cw_ref[0:1, :] + z * cw_ref[1:2, :] + zp1 * cw_ref[2:3, :] + cb_ref[...]
        x0 = u[:, 0:D_HY]
        s = u[:, D_HY:2 * D_HY] * u[:, 2 * D_HY:]
        x0_ref[bb] = x0.astype(BF16)
        p_ref[bb] = (x0 * (s * skip_ref[...])).astype(BF16)
        st_ref[:, bb * t:(bb + 1) * t] = s.T.astype(BF16)

    gate = jnp.dot(hm, w_ref[:, i2:], preferred_element_type=F32)

    per_edge = MXU_EDGE // LANES
    for c in range(2 * D_ATT // MXU_EDGE):
        sl = slice(c * MXU_EDGE, (c + 1) * MXU_EDGE)
        z = za[:, sl]
        zn = z * lax.rsqrt(_head_mean_square(z, bd) + RMS_EPS) * qkg_ref[:, sl]
        for k in range(per_edge):
            stage_ref[c * per_edge + k] = zn[:, k * LANES:(k + 1) * LANES]
    for c in range(2 * D_ATT // LANES, 3 * D_ATT // LANES):
        stage_ref[c] = za[:, c * LANES:(c + 1) * LANES]

    zq = jnp.dot(hm, w_ref[:, i1:i2], preferred_element_type=F32)

    silu = gate * (0.5 * jnp.tanh(0.5 * gate) + 0.5)
    gs_ref[...] = silu.astype(BF16).reshape(nbat, t, D_MIX)

    tq = t // 4
    for c in range(3 * D_ATT // LANES):
        cols = slice(c * LANES, (c + 1) * LANES)
        for bb in range(nbat):
            q1_ref[bb, :, cols] = stage_ref[c, bb * t:(bb + 1) * t, :].astype(BF16)
            tmp = res4_ref.at[c * nbat + bb]
            for r4 in range(4):
                sub = stage_ref[c, pl.ds(bb * t + r4, tq, stride=4), :]
                q4_ref[bb, r4, :, cols] = sub.astype(BF16)
                tmp[r4] = sub
            for r4 in range(4):
                for a in range(4):
                    sub = tmp[r4, pl.ds(a, tq // 4, stride=4), :]
                    q16_ref[bb, 4 * a + r4, :, cols] = sub.astype(BF16)

    zq = zq * lax.rsqrt(_head_mean_square(zq, bd) + RMS_EPS) * mqg_ref[...]
    qm_ref[...] = zq.astype(BF16).reshape(nbat, t, D_MEM)


def _in_proj(x, g, w, bd, qkg, mqg, cw, cb, skip):
    batch, seq, d = x.shape
    d_in = w.shape[1]
    t = SEQ_BLOCK
    nb = seq // t
    nbat = TILE_BATCH
    halo = SUBLANES
    per = t // halo
    n_chunks = 3 * D_ATT // LANES
    const = lambda b, j: (0, 0)
    tok = lambda width: pl.BlockSpec((nbat, t, width), lambda b, j: (b, j, 0))
    res = lambda dil: pl.BlockSpec((nbat, dil, t // dil, 3 * D_ATT), lambda b, j: (b, 0, j, 0))
    return pl.pallas_call(
        functools.partial(_in_proj_kernel, n_blocks=nb),
        grid=(batch // nbat, nb),
        in_specs=[tok(d),
                  pl.BlockSpec((nbat, halo, d), lambda b, j: (b, jnp.maximum(j * per - 1, 0), 0)),
                  pl.BlockSpec((nbat, halo, d),
                               lambda b, j: (b, jnp.minimum((j + 1) * per, seq // halo - 1), 0)),
                  pl.BlockSpec((1, d), const),
                  pl.BlockSpec((d, d_in), const),
                  pl.BlockSpec((MXU_EDGE, MXU_EDGE), const),
                  pl.BlockSpec((1, 2 * D_ATT), const),
                  pl.BlockSpec((1, D_MEM), const),
                  pl.BlockSpec((3, 3 * D_HY), const),
                  pl.BlockSpec((1, 3 * D_HY), const),
                  pl.BlockSpec((1, D_HY), const)],
        out_specs=[pl.BlockSpec((D_HY, nbat * t), lambda b, j: (0, j * (batch // nbat) + b)),
                   tok(D_HY), tok(D_HY), tok(3 * D_ATT), res(4), res(16), tok(D_MEM), tok(D_MIX)],
        out_shape=[jax.ShapeDtypeStruct((D_HY, nb * batch * t), BF16),
                   jax.ShapeDtypeStruct((batch, seq, D_HY), BF16),
                   jax.ShapeDtypeStruct((batch, seq, D_HY), BF16),
                   jax.ShapeDtypeStruct((batch, seq, 3 * D_ATT), BF16),
                   jax.ShapeDtypeStruct((batch, 4, seq // 4, 3 * D_ATT), BF16),
                   jax.ShapeDtypeStruct((batch, 16, seq // 16, 3 * D_ATT), BF16),
                   jax.ShapeDtypeStruct((batch, seq, D_MEM), BF16),
                   jax.ShapeDtypeStruct((batch, seq, D_MIX), BF16)],
        scratch_shapes=[pltpu.VMEM((nbat * (t + 2 * halo), d), BF16),
                        pltpu.VMEM((n_chunks, nbat * t, LANES), F32),
                        pltpu.VMEM((n_chunks * nbat, 4, t // 4, LANES), F32)],
        compiler_params=_params(2),
        name="in_proj",
    )(x, x, x, g, w, bd, qkg, mqg, cw, cb, skip)


def _hyena_filter_kernel(band_ref, w1t_ref, w1c_ref, w1s_ref, b1_ref, fr_ref, w2t_ref, b2_ref,
                         w3f_ref, w3b_ref, dl_ref, g_ref, nrm_ref, ss_ref, *, seq, n_tiles):
    i = pl.program_id(0)
    tl = g_ref.shape[1]
    hi = lax.Precision.HIGHEST
    m = i * tl + lax.broadcasted_iota(jnp.int32, (1, tl), 1)
    lag = m - seq
    pos = jnp.abs(lag).astype(F32)
    tt = pos / float(max(seq - 1, 1))
    ang = ((2.0 * math.pi / seq) * pos) * band_ref[...]
    pre = (w1t_ref[...] * tt
           + jnp.dot(w1c_ref[...], jnp.cos(ang), precision=hi, preferred_element_type=F32)
           + jnp.dot(w1s_ref[...], -jnp.sin(ang), precision=hi, preferred_element_type=F32))
    fr = fr_ref[...]
    h = jnp.sin(fr * (pre + b1_ref[...]))
    h = jnp.sin(fr * (jnp.dot(w2t_ref[...], h, precision=hi, preferred_element_type=F32)
                      + b2_ref[...]))
    fwd = jnp.dot(w3f_ref[...], h, precision=hi, preferred_element_type=F32)
    bwd = jnp.dot(w3b_ref[...], h, precision=hi, preferred_element_type=F32)
    decay = jnp.exp(-tt * dl_ref[...])
    val = jnp.where(lag > 0, fwd, jnp.where(lag < 0, bwd, fwd + bwd)) * decay
    val = jnp.where(m == 0, 0.0, val)
    g_ref[...] = val

    @pl.when(i == 0)
    def _():
        ss_ref[...] = jnp.zeros_like(ss_ref)

    ss_ref[...] += jnp.sum(val * val, axis=1, keepdims=True)

    @pl.when(i == n_tiles - 1)
    def _():
        nrm_ref[...] = lax.rsqrt(ss_ref[...] + 1e-12)


def _hyena_filter(seq, band, w1, b1, fr, w2t, b2, w3f, w3b, deltas):
    tl = min(FILTER_TILE, 2 * seq)
    n_tiles = 2 * seq // tl
    const = lambda i: (0, 0)
    full = lambda a: pl.BlockSpec(a.shape, const)
    w1t = w1.T
    args = (band, w1t[:, 0:1], w1t[:, 1:1 + FILTER_BANDS], w1t[:, 1 + FILTER_BANDS:], b1, fr,
            w2t, b2, w3f, w3b, deltas)
    return pl.pallas_call(
        functools.partial(_hyena_filter_kernel, seq=seq, n_tiles=n_tiles),
        grid=(n_tiles,),
        in_specs=[full(a) for a in args],
        out_specs=[pl.BlockSpec((D_HY, tl), lambda i: (0, i)),
                   pl.BlockSpec((D_HY, 1), const)],
        out_shape=[jax.ShapeDtypeStruct((D_HY, 2 * seq), F32),
                   jax.ShapeDtypeStruct((D_HY, 1), F32)],
        scratch_shapes=[pltpu.VMEM((D_HY, 1), F32)],
        compiler_params=_params(1),
        name="hyena_filter",
    )(*args)


def _expand_filter(g_ref, e_ref, nb):
    t = SEQ_BLOCK
    width = 2 * nb * t
    half = t // 2
    chunk = min(ROLL_CHUNK, width)
    for c in range(width // chunk):
        m0 = c * chunk
        lo = max(m0 - half, 0)
        win = g_ref[:, lo:m0 + chunk]
        rolled = pltpu.roll(jnp.broadcast_to(win, (half, win.shape[1])), 0, axis=1,
                            stride=1, stride_axis=0)
        blk = rolled[:, m0 - lo:].astype(BF16)
        for w in range(chunk // half):
            piece = blk[:, w * half:(w + 1) * half]
            m_up = m0 + w * half
            r_up = (m_up // t) * t
            e_ref[r_up:r_up + half, m_up % t:m_up % t + half] = piece
            m_lo = m_up + half
            if m_lo < width:
                r_lo = (m_lo // t) * t + half
                e_ref[r_lo:r_lo + half, m_lo % t:m_lo % t + half] = piece


def _toeplitz_matmuls(a_ref, e_ref, o_ref, ak_ref, sh_ref, acc_ref, batch, nb):
    t = SEQ_BLOCK
    k = CONV_GROUP
    rows_a = batch * nb
    sh_ref[0:batch, :] = jnp.zeros((batch, t), F32)
    sh_ref[batch:batch + rows_a, :] = a_ref[...].astype(F32)
    sh_ref[batch + rows_a:, :] = jnp.zeros((batch, t), F32)
    a_shift = sh_ref[...].astype(BF16)
    ak_ref[...] = jnp.zeros_like(ak_ref)
    for e in range(k):
        if e % 2 == 0:
            ak_ref[e * batch:e * batch + rows_a, e * t:(e + 1) * t] = a_ref[...]
        else:
            r0 = (e - 1) * batch
            ak_ref[r0:r0 + rows_a + 2 * batch, e * t:(e + 1) * t] = a_shift
    acc_ref[...] = jnp.zeros_like(acc_ref)
    for q0 in range(0, 2 * nb, k):
        e_lo = 1 if q0 == 0 else 0
        d0 = q0 - nb
        j_lo = max(-d0, e_lo)
        j_hi = min(nb - d0, nb + k - 1)
        j_lo -= j_lo % 2
        j_hi += j_hi % 2
        r_out = batch + batch * (j_lo + d0)
        acc_ref[r_out:r_out + batch * (j_hi - j_lo), :] += jnp.dot(
            ak_ref[batch * j_lo:batch * j_hi, e_lo * t:k * t],
            e_ref[(q0 + e_lo) * t:(q0 + k) * t, :], preferred_element_type=F32)
    o_ref[...] = acc_ref[batch:batch + batch * nb, :]


def _hyena_conv_kernel(a_ref, g_ref, o_ref, e0_ref, e1_ref, ak_ref, sh_ref, acc_ref, *,
                       batch, nb):
    cg = a_ref.shape[0]
    _expand_filter(g_ref.at[0], e0_ref, nb)

    def body(kk, carry):
        c0 = 2 * kk
        _expand_filter(g_ref.at[c0 + 1], e1_ref, nb)
        _toeplitz_matmuls(a_ref.at[c0], e0_ref, o_ref.at[c0], ak_ref, sh_ref, acc_ref, batch, nb)
        _expand_filter(g_ref.at[jnp.minimum(c0 + 2, cg - 1)], e0_ref, nb)
        _toeplitz_matmuls(a_ref.at[c0 + 1], e1_ref, o_ref.at[c0 + 1], ak_ref, sh_ref, acc_ref,
                          batch, nb)
        return carry

    lax.fori_loop(0, cg // 2, body, 0)


def _hyena_conv(st3, g3, batch, nb):
    t = SEQ_BLOCK
    rows = nb * batch
    cg = CONV_CHANNELS
    k = CONV_GROUP
    grp = lambda c: (c, 0, 0)
    expand = pltpu.VMEM((2 * nb * t, t), BF16)
    return pl.pallas_call(
        functools.partial(_hyena_conv_kernel, batch=batch, nb=nb),
        grid=(D_HY // cg,),
        in_specs=[pl.BlockSpec((cg, rows, t), grp),
                  pl.BlockSpec((cg, 1, 2 * nb * t), grp)],
        out_specs=pl.BlockSpec((cg, rows, t), grp),
        out_shape=jax.ShapeDtypeStruct((D_HY, rows, t), F32),
        scratch_shapes=[expand, expand,
                        pltpu.VMEM((rows + k * batch, k * t), BF16),
                        pltpu.VMEM((rows + 2 * batch, t), F32),
                        pltpu.VMEM((rows + 2 * batch, t), F32)],
        compiler_params=_params(1),
        name="hyena_conv",
    )(st3, g3)


def _dil_attn_kernel(bucket_ref, relb_ref, q_ref, k_ref, v_ref, kp_ref, vp_ref, kn_ref, vn_ref,
                     o_ref, lse_ref, kall_ref, vall_ref, bias_ref, *, n):
    n_res, tq, _ = q_ref.shape
    side = ATT_SIDE
    first = (pl.program_id(0) == 0) & (pl.program_id(1) == 0) & (pl.program_id(2) == 0)

    @pl.when(first)
    def _():
        bucket = bucket_ref[...]
        for h in range(N_ATT_HEADS):
            bias_ref[h] = jnp.full(bucket.shape, NEG_INF, F32)

        def body(bk, carry):
            hit = bucket == bk
            for h in range(N_ATT_HEADS):
                bias_ref[h] = jnp.where(hit, relb_ref[bk * N_ATT_HEADS + h], bias_ref[h])
            return carry

        lax.fori_loop(0, N_BUCKETS, body, 0)

    i = pl.program_id(2)
    lane = lax.broadcasted_iota(jnp.int32, (Q_BLOCK, LANES), 1)
    low_head = lane < HEAD_DIM
    head_sel = [low_head[0:1, :].astype(BF16), jnp.logical_not(low_head)[0:1, :].astype(BF16)]
    n_sub = tq // Q_BLOCK
    for rr in range(n_res):
        kall_ref[rr, 0:side, :] = kp_ref[rr]
        kall_ref[rr, side:side + tq, :] = k_ref[rr]
        kall_ref[rr, side + tq:, :] = kn_ref[rr]
        vall_ref[rr, 0:side, :] = vp_ref[rr]
        vall_ref[rr, side:side + tq, :] = v_ref[rr]
        vall_ref[rr, side + tq:, :] = vn_ref[rr]
        for j in range(n_sub):
            r0 = j * Q_BLOCK
            at_edge = j == 0 or j == n_sub - 1
            if at_edge:
                kpos = (i * tq + r0 - side
                        + lax.broadcasted_iota(jnp.int32, (1, 2 * Q_BLOCK), 1))
                in_seq = (kpos >= 0) & (kpos < n)
            for hp in range(N_ATT_HEADS // 2):
                cols = slice(hp * LANES, (hp + 1) * LANES)
                q2 = q_ref[rr, r0:r0 + Q_BLOCK, cols]
                k2 = kall_ref[rr, r0:r0 + 2 * Q_BLOCK, cols]
                v2 = vall_ref[rr, r0:r0 + 2 * Q_BLOCK, cols]
                v2e = jnp.concatenate([v2, jnp.ones_like(v2)], axis=1)
                outs = []
                lses = []
                for hh in range(2):
                    qh = q2 * head_sel[hh]
                    s = lax.dot_general(qh, k2, (((1,), (1,)), ((), ())),
                                        preferred_element_type=F32)
                    s = s + bias_ref[2 * hp + hh]
                    if at_edge:
                        s = jnp.where(in_seq, s, NEG_INF)
                    mx = jnp.max(s, axis=-1, keepdims=True)
                    p = jnp.exp(s - mx).astype(BF16)
                    od = jnp.dot(p, v2e, preferred_element_type=F32)
                    den = od[:, LANES:]
                    outs.append(od[:, 0:LANES] * (1.0 / den))
                    lses.append(mx + jnp.log(den))
                o_ref[rr, r0:r0 + Q_BLOCK, cols] = jnp.where(low_head, outs[0], outs[1])
                lse_ref[rr, r0:r0 + Q_BLOCK, cols] = jnp.where(low_head, lses[0], lses[1])


def _attn_bucket_table(dil):
    half = N_BUCKETS // 2
    max_exact = half // 2
    qi = np.arange(Q_BLOCK)[:, None]
    kj = np.arange(2 * Q_BLOCK)[None, :]
    rel_steps = kj - ATT_SIDE - qi
    rel = rel_steps * dil
    ret = np.where(rel > 0, half, 0)
    na = np.abs(rel)
    large = max_exact + (np.log(np.maximum(na, 1).astype(np.float32) / np.float32(max_exact))
                         / np.float32(math.log(MAX_DISTANCE / max_exact))
                         * np.float32(half - max_exact)).astype(np.int32)
    large = np.minimum(large, half - 1)
    bucket = ret + np.where(na < max_exact, na, large)
    return np.where(np.abs(rel_steps) <= ATT_SIDE, bucket, -1).astype(np.int32)


def _dil_attn(qkv, relb_flat, dil):
    batch, _, n, _ = qkv.shape
    tq = min(n, ATT_ROWS)
    nt = n // tq
    n_res = min(dil, ATT_ROWS // tq)
    side = ATT_SIDE
    per = tq // side
    bucket = jnp.asarray(_attn_bucket_table(dil))
    main = lambda c: (lambda b, r, i: (b, r, i, c))
    prev = lambda c: (lambda b, r, i: (b, r, jnp.maximum(i * per - 1, 0), c))
    nxt = lambda c: (lambda b, r, i: (b, r, jnp.minimum((i + 1) * per, n // side - 1), c))
    blk = lambda rows, imap: pl.BlockSpec((None, n_res, rows, D_ATT), imap)
    return pl.pallas_call(
        functools.partial(_dil_attn_kernel, n=n),
        grid=(batch, dil // n_res, nt),
        in_specs=[pl.BlockSpec(bucket.shape, lambda b, r, i: (0, 0)),
                  pl.BlockSpec(memory_space=pltpu.SMEM),
                  blk(tq, main(0)), blk(tq, main(1)), blk(tq, main(2)),
                  blk(side, prev(1)), blk(side, prev(2)),
                  blk(side, nxt(1)), blk(side, nxt(2))],
        out_specs=[blk(tq, main(0)), blk(tq, main(0))],
        out_shape=[jax.ShapeDtypeStruct((batch, dil, n, D_ATT), F32),
                   jax.ShapeDtypeStruct((batch, dil, n, D_ATT), F32)],
        scratch_shapes=[pltpu.VMEM((n_res, tq + 2 * side, D_ATT), BF16),
                        pltpu.VMEM((n_res, tq + 2 * side, D_ATT), BF16),
                        pltpu.VMEM((N_ATT_HEADS, Q_BLOCK, 2 * Q_BLOCK), F32)],
        compiler_params=_params(3),
        name=f"dil_attn_d{dil}",
    )(bucket, relb_flat, qkv, qkv, qkv, qkv, qkv, qkv, qkv)


def _mem_kv_kernel(m_ref, g_ref, w_ref, bd_ref, kg_ref, kv_ref):
    x = m_ref[...]
    ms = jnp.mean(x * x, axis=-1, keepdims=True)
    h = (x * lax.rsqrt(ms + RMS_EPS) * g_ref[...]).astype(BF16)
    kv = jnp.dot(h, w_ref[...], preferred_element_type=F32)
    k = kv[:, 0:D_MEM]
    kn = k * lax.rsqrt(_head_mean_square(k, bd_ref[...]) + RMS_EPS) * kg_ref[...]
    kv_ref[:, 0:D_MEM] = kn.astype(BF16)
    kv_ref[:, D_MEM:] = kv[:, D_MEM:].astype(BF16)


def _mem_kv(mem, g, w, bd, kg):
    batch, n_mem, d = mem.shape
    const = lambda b: (0, 0)
    return pl.pallas_call(
        _mem_kv_kernel,
        grid=(batch,),
        in_specs=[pl.BlockSpec((None, n_mem, d), lambda b: (b, 0, 0)),
                  pl.BlockSpec((1, d), const),
                  pl.BlockSpec((d, 2 * D_MEM), const),
                  pl.BlockSpec((MXU_EDGE, MXU_EDGE), const),
                  pl.BlockSpec((1, D_MEM), const)],
        out_specs=pl.BlockSpec((None, n_mem, 2 * D_MEM), lambda b: (b, 0, 0)),
        out_shape=jax.ShapeDtypeStruct((batch, n_mem, 2 * D_MEM), BF16),
        compiler_params=_params(1),
        name="mem_kv",
    )(mem, g, w, bd, kg)


def _out_proj_kernel(x_ref, ct_ref, nrm_ref, x0_ref, p_ref, o1_ref, l1_ref, o4_ref, l4_ref,
                     o16_ref, l16_ref, qm_ref, kv_ref, gs_ref, w_ref, y_ref,
                     so4_ref, sl4_ref, so16_ref, sl16_ref, tmp_ref, mixed_ref):
    nbat, t, d = x_ref.shape
    rows = nbat * t

    def gated(value, cols):
        gate = gs_ref[:, :, cols].astype(F32).reshape(rows, cols.stop - cols.start)
        mixed_ref[:, cols] = (value * gate).astype(BF16)

    conv = (ct_ref[...] * nrm_ref[...]).T
    y_hy = (x0_ref[...].astype(F32).reshape(rows, D_HY) * conv
            + p_ref[...].astype(F32).reshape(rows, D_HY))
    gated(y_hy, slice(0, D_HY))

    tq = t // 4
    for c in range(D_ATT // LANES):
        cols = slice(c * LANES, (c + 1) * LANES)
        for bb in range(nbat):
            for k, (src4, src16, dst4, dst16) in enumerate(((o4_ref, o16_ref, so4_ref, so16_ref),
                                                           (l4_ref, l16_ref, sl4_ref, sl16_ref))):
                tmp = tmp_ref.at[(c * nbat + bb) * 2 + k]
                for r4 in range(4):
                    dst4[c, pl.ds(bb * t + r4, tq, stride=4), :] = src4[bb, r4, :, cols]
                    for a in range(4):
                        tmp[r4, pl.ds(a, tq // 4, stride=4), :] = src16[bb, 4 * a + r4, :, cols]
                for r4 in range(4):
                    dst16[c, pl.ds(bb * t + r4, tq, stride=4), :] = tmp[r4]
        l1 = l1_ref[:, :, cols].reshape(rows, LANES)
        l2, l3 = sl4_ref[c], sl16_ref[c]
        top = jnp.maximum(jnp.maximum(l1, l2), l3)
        e1, e2, e3 = jnp.exp(l1 - top), jnp.exp(l2 - top), jnp.exp(l3 - top)
        y_att = ((e1 * o1_ref[:, :, cols].reshape(rows, LANES) + e2 * so4_ref[c]
                  + e3 * so16_ref[c]) * (1.0 / (e1 + e2 + e3)))
        gated(y_att, slice(D_HY + c * LANES, D_HY + (c + 1) * LANES))

    lane = lax.broadcasted_iota(jnp.int32, (t, LANES), 1)
    low_head = lane < HEAD_DIM
    head_sel = [low_head[0:1, :].astype(BF16), jnp.logical_not(low_head)[0:1, :].astype(BF16)]
    for hp in range(N_MEM_HEADS // 2):
        cols = slice(hp * LANES, (hp + 1) * LANES)
        mem_rows = []
        for bb in range(nbat):
            q2 = qm_ref[bb, :, cols]
            k2 = kv_ref[bb, :, cols]
            v2 = kv_ref[bb, :, D_MEM + hp * LANES:D_MEM + (hp + 1) * LANES]
            v2e = jnp.concatenate([v2, jnp.ones_like(v2)], axis=1)
            outs = []
            for hh in range(2):
                qh = q2 * head_sel[hh]
                s = lax.dot_general(qh, k2, (((1,), (1,)), ((), ())),
                                    preferred_element_type=F32)
                mx = jnp.max(s, axis=-1, keepdims=True)
                p = jnp.exp(s - mx).astype(BF16)
                od = jnp.dot(p, v2e, preferred_element_type=F32)
                outs.append(od[:, 0:LANES] * (1.0 / od[:, LANES:]))
            mem_rows.append(jnp.where(low_head, outs[0], outs[1]))
        base = D_HY + D_ATT + hp * LANES
        gated(jnp.concatenate(mem_rows, axis=0), slice(base, base + LANES))

    y = x_ref[...].reshape(rows, d) + jnp.dot(mixed_ref[...], w_ref[...],
                                              preferred_element_type=F32)
    y_ref[...] = y.reshape(nbat, t, d)


def _out_proj(x, conv_t, nrm, x0, p, att, qm, kv, gs, w):
    batch, seq, d = x.shape
    t = SEQ_BLOCK
    nb = seq // t
    nbat = TILE_BATCH
    n_chunks = D_ATT // LANES
    const = lambda b, j: (0, 0)
    tok = lambda width: pl.BlockSpec((nbat, t, width), lambda b, j: (b, j, 0))
    res = lambda dil: pl.BlockSpec((nbat, dil, t // dil, D_ATT), lambda b, j: (b, 0, j, 0))
    nat = pl.BlockSpec((nbat, None, t, D_ATT), lambda b, j: (b, 0, j, 0))
    (o1, l1), (o4, l4), (o16, l16) = att
    natural = pltpu.VMEM((n_chunks, nbat * t, LANES), F32)
    return pl.pallas_call(
        _out_proj_kernel,
        grid=(batch // nbat, nb),
        in_specs=[tok(d),
                  pl.BlockSpec((D_HY, nbat * t), lambda b, j: (0, j * (batch // nbat) + b)),
                  pl.BlockSpec((D_HY, 1), const),
                  tok(D_HY), tok(D_HY), nat, nat, res(4), res(4), res(16), res(16),
                  tok(D_MEM),
                  pl.BlockSpec((nbat, N_MEM, 2 * D_MEM), lambda b, j: (b, 0, 0)),
                  tok(D_MIX),
                  pl.BlockSpec((D_MIX, d), const)],
        out_specs=tok(d),
        out_shape=jax.ShapeDtypeStruct((batch, seq, d), F32),
        scratch_shapes=[natural, natural, natural, natural,
                        pltpu.VMEM((n_chunks * nbat * 2, 4, t // 4, LANES), F32),
                        pltpu.VMEM((nbat * t, D_MIX), BF16)],
        compiler_params=_params(2),
        name="out_proj",
    )(x, conv_t, nrm, x0, p, o1, l1, o4, l4, o16, l16, qm, kv, gs, w)


def _encoder_layer(x, mem, norm_in, w_in, hy_conv_w, hy_conv_b, hy_filt_w1, hy_filt_b1,
                   hy_filt_freq, hy_filt_w2, hy_filt_b2, hy_filt_w3, hy_skip, att_q_norm,
                   att_k_norm, mem_norm, w_mem_kv, mem_q_norm, mem_k_norm, w_out, rel_bias):
    batch, seq, d = x.shape
    assert seq % (SEQ_BLOCK * 2) == 0 and batch % TILE_BATCH == 0 and batch % SUBLANES == 0
    assert all(seq // dil >= Q_BLOCK for _, dil in DILATED_CONFIGS)
    assert all(win // (2 * dil) == ATT_SIDE for win, dil in DILATED_CONFIGS)
    assert tuple(dil for _, dil in DILATED_CONFIGS) == (1, 4, 16)
    nb = seq // SEQ_BLOCK
    scale = HEAD_DIM ** -0.5

    blk = np.arange(MXU_EDGE) // HEAD_DIM
    bd = jnp.asarray((blk[:, None] == blk[None, :]).astype(np.float32) / HEAD_DIM, BF16)
    qkg = jnp.concatenate([jnp.tile(att_q_norm.astype(F32), N_ATT_HEADS) * scale,
                           jnp.tile(att_k_norm.astype(F32), N_ATT_HEADS)])[None, :]
    mqg = (jnp.tile(mem_q_norm.astype(F32), N_MEM_HEADS) * scale)[None, :]
    mkg = jnp.tile(mem_k_norm.astype(F32), N_MEM_HEADS)[None, :]

    st, x0, p, q1, q4, q16, qm, gs = _in_proj(
        x, norm_in[None, :], w_in.astype(BF16), bd, qkg, mqg, hy_conv_w.astype(F32),
        hy_conv_b.astype(F32)[None, :], hy_skip.astype(F32)[None, :])

    band = jnp.linspace(1e-4, FILTER_BANDS - 1, FILTER_BANDS, dtype=F32)[:, None]
    deltas = jnp.abs(jnp.linspace(math.log(DECAY_TARGET) / SLOW_DECAY_PCT,
                                  math.log(DECAY_TARGET) / FAST_DECAY_PCT, D_HY, dtype=F32))
    w3t = hy_filt_w3.astype(F32).T
    g, nrm = _hyena_filter(seq, band, hy_filt_w1.astype(F32), hy_filt_b1.astype(F32)[:, None],
                           hy_filt_freq.astype(F32)[:, None], hy_filt_w2.astype(F32).T,
                           hy_filt_b2.astype(F32)[:, None], w3t[0:D_HY], w3t[D_HY:],
                           deltas[:, None])

    conv_t = _hyena_conv(st.reshape(D_HY, nb * batch, SEQ_BLOCK),
                         g.reshape(D_HY, 1, 2 * seq), batch, nb)
    conv_t = conv_t.reshape(D_HY, nb * batch * SEQ_BLOCK)

    relb_flat = rel_bias.astype(F32).reshape(-1)
    att = [_dil_attn(q1.reshape(batch, 1, seq, 3 * D_ATT), relb_flat, 1),
           _dil_attn(q4, relb_flat, 4),
           _dil_attn(q16, relb_flat, 16)]

    kv = _mem_kv(mem, mem_norm[None, :], w_mem_kv.astype(BF16), bd, mkg)

    return _out_proj(x, conv_t, nrm, x0, p, att, qm, kv, gs, w_out.astype(BF16))


def kernel(x_prompt, x_sample, mem_prompt, mem_sample, norm_in, w_in, hy_conv_w, hy_conv_b, hy_filt_w1, hy_filt_b1, hy_filt_freq, hy_filt_w2, hy_filt_b2, hy_filt_w3, hy_skip, att_q_norm, att_k_norm, mem_norm, w_mem_kv, mem_q_norm, mem_k_norm, w_out, rel_bias):
    def trunk(x, mem):
        for l in range(norm_in.shape[0]):
            x = _encoder_layer(x, mem, norm_in[l], w_in[l], hy_conv_w[l], hy_conv_b[l],
                               hy_filt_w1[l], hy_filt_b1[l], hy_filt_freq[l], hy_filt_w2[l],
                               hy_filt_b2[l], hy_filt_w3[l], hy_skip[l], att_q_norm[l],
                               att_k_norm[l], mem_norm[l], w_mem_kv[l], mem_q_norm[l],
                               mem_k_norm[l], w_out[l], rel_bias)
        return x

    return (trunk(x_prompt, mem_prompt), trunk(x_sample, mem_sample))
```

```python
import functools
import math

import numpy as np
import jax
import jax.numpy as jnp
from jax import lax
from jax.experimental import pallas as pl
from jax.experimental.pallas import tpu as pltpu

F32 = jnp.float32
BF16 = jnp.bfloat16

N_MEM = 256
HEAD_DIM = 64
D_HY = 384
N_ATT_HEADS = 6
D_ATT = N_ATT_HEADS * HEAD_DIM
N_MEM_HEADS = 4
D_MEM = N_MEM_HEADS * HEAD_DIM
D_MIX = D_HY + D_ATT + D_MEM
FILTER_BANDS = 16
FILTER_HIDDEN = 64
DECAY_TARGET = 1e-2
FAST_DECAY_PCT = 0.3
SLOW_DECAY_PCT = 1.5
DILATED_CONFIGS = ((128, 1), (512, 4), (2048, 16))
N_BUCKETS = 32
MAX_DISTANCE = 1024
RMS_EPS = 1e-6
NEG_INF = -1e30

SUBLANES = 8
LANES = 128
MXU_EDGE = 256

SEQ_BLOCK = MXU_EDGE
TILE_BATCH = 2
Q_BLOCK = 128
ATT_SIDE = 64
ATT_ROWS = 1024
FILTER_TILE = 2048
ROLL_CHUNK = 1024
CONV_CHANNELS = 16
CONV_GROUP = 4
VMEM_LIMIT = 56 * 1024 * 1024


def _params(n_axes):
    return pltpu.CompilerParams(dimension_semantics=("arbitrary",) * n_axes,
                                vmem_limit_bytes=VMEM_LIMIT)


def _head_mean_square(z, bd):
    return jnp.dot((z * z).astype(BF16), bd, preferred_element_type=F32)


def _in_proj_kernel(x_ref, xp_ref, xn_ref, g_ref, w_ref, bd_ref, qkg_ref, mqg_ref,
                    cw_ref, cb_ref, skip_ref,
                    st_ref, x0_ref, p_ref, q1_ref, q4_ref, q16_ref, qm_ref, gs_ref,
                    h_ref, stage_ref, res4_ref, *, n_blocks):
    i0 = 3 * D_HY
    i1 = i0 + 3 * D_ATT
    i2 = i1 + D_MEM
    j = pl.program_id(1)
    nbat, t, d = x_ref.shape
    halo = xp_ref.shape[1]
    rows = nbat * t

    def normed(xv):
        ms = jnp.mean(xv * xv, axis=-1, keepdims=True)
        return (xv * lax.rsqrt(ms + RMS_EPS) * g_ref[...]).astype(BF16)

    h_ref[0:rows, :] = normed(x_ref[...].reshape(rows, d))
    h_ref[rows:rows + nbat * halo, :] = normed(xp_ref[...].reshape(nbat * halo, d))
    h_ref[rows + nbat * halo:, :] = normed(xn_ref[...].reshape(nbat * halo, d))
    hm = h_ref[0:rows, :]
    bd = bd_ref[...]

    zall = jnp.dot(h_ref[...], w_ref[:, 0:i0], preferred_element_type=F32)
    za = jnp.dot(hm, w_ref[:, i0:i1], preferred_element_type=F32)

    not_first = (j > 0).astype(F32)
    not_last = (j < n_blocks - 1).astype(F32)
    row = lax.broadcasted_iota(jnp.int32, (t, i0), 0)
    for bb in range(nbat):
        z = zall[bb * t:(bb + 1) * t]
        r_before = rows + bb * halo + halo - 1
        r_after = rows + nbat * halo + bb * halo
        z_before = zall[r_before:r_before + 1] * not_first
        z_after = zall[r_after:r_after + 1] * not_last
        zm1 = jnp.where(row == 0, z_before, pltpu.roll(z, 1, axis=0))
        zp1 = jnp.where(row == t - 1, z_after, pltpu.roll(z, t - 1, axis=0))
        u = zm1 * cw_ref[0:1, :] + z * cw_ref[1:2, :] + zp1 * cw_ref[2:3, :] + cb_ref[...]
        x0 = u[:, 0:D_HY]
        s = u[:, D_HY:2 * D_HY] * u[:, 2 * D_HY:]
        x0_ref[bb] = x0.astype(BF16)
        p_ref[bb] = (x0 * (s * skip_ref[...])).astype(BF16)
        st_ref[:, bb * t:(bb + 1) * t] = s.T.astype(BF16)

    gate = jnp.dot(hm, w_ref[:, i2:], preferred_element_type=F32)

    per_edge = MXU_EDGE // LANES
    for c in range(2 * D_ATT // MXU_EDGE):
        sl = slice(c * MXU_EDGE, (c + 1) * MXU_EDGE)
        z = za[:, sl]
        zn = z * lax.rsqrt(_head_mean_square(z, bd) + RMS_EPS) * qkg_ref[:, sl]
        for k in range(per_edge):
            stage_ref[c * per_edge + k] = zn[:, k * LANES:(k + 1) * LANES]
    for c in range(2 * D_ATT // LANES, 3 * D_ATT // LANES):
        stage_ref[c] = za[:, c * LANES:(c + 1) * LANES]

    zq = jnp.dot(hm, w_ref[:, i1:i2], preferred_element_type=F32)

    half_gate = 0.5 * gate
    silu = half_gate * jnp.tanh(half_gate) + half_gate
    gs_ref[...] = silu.astype(BF16).reshape(nbat, t, D_MIX)

    tq = t // 4
    for c in range(3 * D_ATT // LANES):
        cols = slice(c * LANES, (c + 1) * LANES)
        for bb in range(nbat):
            q1_ref[bb, :, cols] = stage_ref[c, bb * t:(bb + 1) * t, :].astype(BF16)
            tmp = res4_ref.at[c * nbat + bb]
            for r4 in range(4):
                sub = stage_ref[c, pl.ds(bb * t + r4, tq, stride=4), :]
                q4_ref[bb, r4, :, cols] = sub.astype(BF16)
                tmp[r4] = sub
            for r4 in range(4):
                for a in range(4):
                    sub = tmp[r4, pl.ds(a, tq // 4, stride=4), :]
                    q16_ref[bb, 4 * a + r4, :, cols] = sub.astype(BF16)

    zq = zq * lax.rsqrt(_head_mean_square(zq, bd) + RMS_EPS) * mqg_ref[...]
    qm_ref[...] = zq.astype(BF16).reshape(nbat, t, D_MEM)


def _in_proj(x, g, w, bd, qkg, mqg, cw, cb, skip):
    batch, seq, d = x.shape
    d_in = w.shape[1]
    t = SEQ_BLOCK
    nb = seq // t
    nbat = TILE_BATCH
    halo = SUBLANES
    per = t // halo
    n_chunks = 3 * D_ATT // LANES
    const = lambda b, j: (0, 0)
    tok = lambda width: pl.BlockSpec((nbat, t, width), lambda b, j: (b, j, 0))
    res = lambda dil: pl.BlockSpec((nbat, dil, t // dil, 3 * D_ATT), lambda b, j: (b, 0, j, 0))
    return pl.pallas_call(
        functools.partial(_in_proj_kernel, n_blocks=nb),
        grid=(batch // nbat, nb),
        in_specs=[tok(d),
                  pl.BlockSpec((nbat, halo, d), lambda b, j: (b, jnp.maximum(j * per - 1, 0), 0)),
                  pl.BlockSpec((nbat, halo, d),
                               lambda b, j: (b, jnp.minimum((j + 1) * per, seq // halo - 1), 0)),
                  pl.BlockSpec((1, d), const),
                  pl.BlockSpec((d, d_in), const),
                  pl.BlockSpec((MXU_EDGE, MXU_EDGE), const),
                  pl.BlockSpec((1, 2 * D_ATT), const),
                  pl.BlockSpec((1, D_MEM), const),
                  pl.BlockSpec((3, 3 * D_HY), const),
                  pl.BlockSpec((1, 3 * D_HY), const),
                  pl.BlockSpec((1, D_HY), const)],
        out_specs=[pl.BlockSpec((D_HY, nbat * t), lambda b, j: (0, j * (batch // nbat) + b)),
                   tok(D_HY), tok(D_HY), tok(3 * D_ATT), res(4), res(16), tok(D_MEM), tok(D_MIX)],
        out_shape=[jax.ShapeDtypeStruct((D_HY, nb * batch * t), BF16),
                   jax.ShapeDtypeStruct((batch, seq, D_HY), BF16),
                   jax.ShapeDtypeStruct((batch, seq, D_HY), BF16),
                   jax.ShapeDtypeStruct((batch, seq, 3 * D_ATT), BF16),
                   jax.ShapeDtypeStruct((batch, 4, seq // 4, 3 * D_ATT), BF16),
                   jax.ShapeDtypeStruct((batch, 16, seq // 16, 3 * D_ATT), BF16),
                   jax.ShapeDtypeStruct((batch, seq, D_MEM), BF16),
                   jax.ShapeDtypeStruct((batch, seq, D_MIX), BF16)],
        scratch_shapes=[pltpu.VMEM((nbat * (t + 2 * halo), d), BF16),
                        pltpu.VMEM((n_chunks, nbat * t, LANES), F32),
                        pltpu.VMEM((n_chunks * nbat, 4, t // 4, LANES), F32)],
        compiler_params=_params(2),
        name="in_proj",
    )(x, x, x, g, w, bd, qkg, mqg, cw, cb, skip)


def _hyena_filter_kernel(band_ref, w1t_ref, w1c_ref, w1s_ref, b1_ref, fr_ref, w2t_ref, b2_ref,
                         w3f_ref, w3b_ref, dl_ref, g_ref, nrm_ref, ss_ref, *, seq, n_tiles):
    i = pl.program_id(0)
    tl = g_ref.shape[1]
    hi = lax.Precision.HIGHEST
    m = i * tl + lax.broadcasted_iota(jnp.int32, (1, tl), 1)
    lag = m - seq
    pos = jnp.abs(lag).astype(F32)
    tt = pos / float(max(seq - 1, 1))
    ang = ((2.0 * math.pi / seq) * pos) * band_ref[...]
    pre = (w1t_ref[...] * tt
           + jnp.dot(w1c_ref[...], jnp.cos(ang), precision=hi, preferred_element_type=F32)
           + jnp.dot(w1s_ref[...], -jnp.sin(ang), precision=hi, preferred_element_type=F32))
    fr = fr_ref[...]
    h = jnp.sin(fr * (pre + b1_ref[...]))
    h = jnp.sin(fr * (jnp.dot(w2t_ref[...], h, precision=hi, preferred_element_type=F32)
                      + b2_ref[...]))
    fwd = jnp.dot(w3f_ref[...], h, precision=hi, preferred_element_type=F32)
    bwd = jnp.dot(w3b_ref[...], h, precision=hi, preferred_element_type=F32)
    decay = jnp.exp(-tt * dl_ref[...])
    val = jnp.where(lag > 0, fwd, jnp.where(lag < 0, bwd, fwd + bwd)) * decay
    val = jnp.where(m == 0, 0.0, val)
    g_ref[...] = val

    @pl.when(i == 0)
    def _():
        ss_ref[...] = jnp.zeros_like(ss_ref)

    ss_ref[...] += jnp.sum(val * val, axis=1, keepdims=True)

    @pl.when(i == n_tiles - 1)
    def _():
        nrm_ref[...] = lax.rsqrt(ss_ref[...] + 1e-12)


def _hyena_filter(seq, band, w1, b1, fr, w2t, b2, w3f, w3b, deltas):
    tl = min(FILTER_TILE, 2 * seq)
    n_tiles = 2 * seq // tl
    const = lambda i: (0, 0)
    full = lambda a: pl.BlockSpec(a.shape, const)
    w1t = w1.T
    args = (band, w1t[:, 0:1], w1t[:, 1:1 + FILTER_BANDS], w1t[:, 1 + FILTER_BANDS:], b1, fr,
            w2t, b2, w3f, w3b, deltas)
    return pl.pallas_call(
        functools.partial(_hyena_filter_kernel, seq=seq, n_tiles=n_tiles),
        grid=(n_tiles,),
        in_specs=[full(a) for a in args],
        out_specs=[pl.BlockSpec((D_HY, tl), lambda i: (0, i)),
                   pl.BlockSpec((D_HY, 1), const)],
        out_shape=[jax.ShapeDtypeStruct((D_HY, 2 * seq), F32),
                   jax.ShapeDtypeStruct((D_HY, 1), F32)],
        scratch_shapes=[pltpu.VMEM((D_HY, 1), F32)],
        compiler_params=_params(1),
        name="hyena_filter",
    )(*args)


def _expand_filter(g_ref, e_ref, nb):
    t = SEQ_BLOCK
    width = 2 * nb * t
    half = t // 2
    chunk = min(ROLL_CHUNK, width)
    for c in range(width // chunk):
        m0 = c * chunk
        lo = max(m0 - half, 0)
        win = g_ref[:, lo:m0 + chunk]
        rolled = pltpu.roll(jnp.broadcast_to(win, (half, win.shape[1])), 0, axis=1,
                            stride=1, stride_axis=0)
        blk = rolled[:, m0 - lo:].astype(BF16)
        for w in range(chunk // half):
            piece = blk[:, w * half:(w + 1) * half]
            m_up = m0 + w * half
            r_up = (m_up // t) * t
            e_ref[r_up:r_up + half, m_up % t:m_up % t + half] = piece
            m_lo = m_up + half
            if m_lo < width:
                r_lo = (m_lo // t) * t + half
                e_ref[r_lo:r_lo + half, m_lo % t:m_lo % t + half] = piece


def _toeplitz_matmuls(a_ref, e_ref, o_ref, ak_ref, sh_ref, acc_ref, batch, nb):
    t = SEQ_BLOCK
    k = CONV_GROUP
    rows_a = batch * nb
    sh_ref[0:batch, :] = jnp.zeros((batch, t), F32)
    sh_ref[batch:batch + rows_a, :] = a_ref[...].astype(F32)
    sh_ref[batch + rows_a:, :] = jnp.zeros((batch, t), F32)
    a_shift = sh_ref[...].astype(BF16)
    ak_ref[...] = jnp.zeros_like(ak_ref)
    for e in range(k):
        if e % 2 == 0:
            ak_ref[e * batch:e * batch + rows_a, e * t:(e + 1) * t] = a_ref[...]
        else:
            r0 = (e - 1) * batch
            ak_ref[r0:r0 + rows_a + 2 * batch, e * t:(e + 1) * t] = a_shift
    acc_ref[...] = jnp.zeros_like(acc_ref)
    for q0 in range(0, 2 * nb, k):
        e_lo = 1 if q0 == 0 else 0
        d0 = q0 - nb
        j_lo = max(-d0, e_lo)
        j_hi = min(nb - d0, nb + k - 1)
        j_lo -= j_lo % 2
        j_hi += j_hi % 2
        r_out = batch + batch * (j_lo + d0)
        acc_ref[r_out:r_out + batch * (j_hi - j_lo), :] += jnp.dot(
            ak_ref[batch * j_lo:batch * j_hi, e_lo * t:k * t],
            e_ref[(q0 + e_lo) * t:(q0 + k) * t, :], preferred_element_type=F32)
    o_ref[...] = acc_ref[batch:batch + batch * nb, :].astype(o_ref.dtype)


def _hyena_conv_kernel(a_ref, g_ref, o_ref, e0_ref, e1_ref, ak_ref, sh_ref, acc_ref, *,
                       batch, nb):
    cg = a_ref.shape[0]
    _expand_filter(g_ref.at[0], e0_ref, nb)

    def body(kk, carry):
        c0 = 2 * kk
        _expand_filter(g_ref.at[c0 + 1], e1_ref, nb)
        _toeplitz_matmuls(a_ref.at[c0], e0_ref, o_ref.at[c0], ak_ref, sh_ref, acc_ref, batch, nb)
        _expand_filter(g_ref.at[jnp.minimum(c0 + 2, cg - 1)], e0_ref, nb)
        _toeplitz_matmuls(a_ref.at[c0 + 1], e1_ref, o_ref.at[c0 + 1], ak_ref, sh_ref, acc_ref,
                          batch, nb)
        return carry

    lax.fori_loop(0, cg // 2, body, 0)


def _hyena_conv(st3, g3, batch, nb):
    t = SEQ_BLOCK
    rows = nb * batch
    cg = CONV_CHANNELS
    k = CONV_GROUP
    grp = lambda c: (c, 0, 0)
    expand = pltpu.VMEM((2 * nb * t, t), BF16)
    return pl.pallas_call(
        functools.partial(_hyena_conv_kernel, batch=batch, nb=nb),
        grid=(D_HY // cg,),
        in_specs=[pl.BlockSpec((cg, rows, t), grp),
                  pl.BlockSpec((cg, 1, 2 * nb * t), grp)],
        out_specs=pl.BlockSpec((cg, rows, t), grp),
        out_shape=jax.ShapeDtypeStruct((D_HY, rows, t), BF16),
        scratch_shapes=[expand, expand,
                        pltpu.VMEM((rows + k * batch, k * t), BF16),
                        pltpu.VMEM((rows + 2 * batch, t), F32),
                        pltpu.VMEM((rows + 2 * batch, t), F32)],
        compiler_params=_params(1),
        name="hyena_conv",
    )(st3, g3)


def _dil_attn_kernel(bucket_ref, relb_ref, q_ref, k_ref, v_ref, kp_ref, vp_ref, kn_ref, vn_ref,
                     o_ref, lse_ref, kall_ref, vall_ref, bias_ref, *, n):
    n_res, tq, _ = q_ref.shape
    side = ATT_SIDE
    first = (pl.program_id(0) == 0) & (pl.program_id(1) == 0) & (pl.program_id(2) == 0)

    @pl.when(first)
    def _():
        bucket = bucket_ref[...]
        for h in range(N_ATT_HEADS):
            bias_ref[h] = jnp.full(bucket.shape, NEG_INF, F32)

        def body(bk, carry):
            hit = bucket == bk
            for h in range(N_ATT_HEADS):
                bias_ref[h] = jnp.where(hit, relb_ref[bk * N_ATT_HEADS + h], bias_ref[h])
            return carry

        lax.fori_loop(0, N_BUCKETS, body, 0)

    i = pl.program_id(2)
    lane = lax.broadcasted_iota(jnp.int32, (Q_BLOCK, LANES), 1)
    low_head = lane < HEAD_DIM
    head_sel = [low_head[0:1, :].astype(BF16), jnp.logical_not(low_head)[0:1, :].astype(BF16)]
    n_sub = tq // Q_BLOCK
    for rr in range(n_res):
        kall_ref[rr, 0:side, :] = kp_ref[rr]
        kall_ref[rr, side:side + tq, :] = k_ref[rr]
        kall_ref[rr, side + tq:, :] = kn_ref[rr]
        vall_ref[rr, 0:side, :] = vp_ref[rr]
        vall_ref[rr, side:side + tq, :] = v_ref[rr]
        vall_ref[rr, side + tq:, :] = vn_ref[rr]
        for j in range(n_sub):
            r0 = j * Q_BLOCK
            at_edge = j == 0 or j == n_sub - 1
            if at_edge:
                kpos = (i * tq + r0 - side
                        + lax.broadcasted_iota(jnp.int32, (1, 2 * Q_BLOCK), 1))
                in_seq = (kpos >= 0) & (kpos < n)
            for hp in range(N_ATT_HEADS // 2):
                cols = slice(hp * LANES, (hp + 1) * LANES)
                q2 = q_ref[rr, r0:r0 + Q_BLOCK, cols]
                k2 = kall_ref[rr, r0:r0 + 2 * Q_BLOCK, cols]
                v2 = vall_ref[rr, r0:r0 + 2 * Q_BLOCK, cols]
                v2e = jnp.concatenate([v2, jnp.ones_like(v2)], axis=1)
                outs = []
                lses = []
                for hh in range(2):
                    qh = q2 * head_sel[hh]
                    s = lax.dot_general(qh, k2, (((1,), (1,)), ((), ())),
                                        preferred_element_type=F32)
                    s = s + bias_ref[2 * hp + hh]
                    if at_edge:
                        s = jnp.where(in_seq, s, NEG_INF)
                    mx = jnp.max(s, axis=-1, keepdims=True)
                    p = jnp.exp(s - mx).astype(BF16)
                    od = jnp.dot(p, v2e, preferred_element_type=F32)
                    den = od[:, LANES:]
                    outs.append(od[:, 0:LANES] * (1.0 / den))
                    lses.append(mx + jnp.log(den))
                o_ref[rr, r0:r0 + Q_BLOCK, cols] = jnp.where(
                    low_head, outs[0], outs[1]).astype(BF16)
                lse_ref[rr, r0:r0 + Q_BLOCK, cols] = jnp.where(low_head, lses[0], lses[1])


def _attn_bucket_table(dil):
    half = N_BUCKETS // 2
    max_exact = half // 2
    qi = np.arange(Q_BLOCK)[:, None]
    kj = np.arange(2 * Q_BLOCK)[None, :]
    rel_steps = kj - ATT_SIDE - qi
    rel = rel_steps * dil
    ret = np.where(rel > 0, half, 0)
    na = np.abs(rel)
    large = max_exact + (np.log(np.maximum(na, 1).astype(np.float32) / np.float32(max_exact))
                         / np.float32(math.log(MAX_DISTANCE / max_exact))
                         * np.float32(half - max_exact)).astype(np.int32)
    large = np.minimum(large, half - 1)
    bucket = ret + np.where(na < max_exact, na, large)
    return np.where(np.abs(rel_steps) <= ATT_SIDE, bucket, -1).astype(np.int32)


def _dil_attn(qkv, relb_flat, dil):
    batch, _, n, _ = qkv.shape
    tq = min(n, ATT_ROWS)
    nt = n // tq
    n_res = min(dil, ATT_ROWS // tq)
    side = ATT_SIDE
    per = tq // side
    bucket = jnp.asarray(_attn_bucket_table(dil))
    main = lambda c: (lambda b, r, i: (b, r, i, c))
    prev = lambda c: (lambda b, r, i: (b, r, jnp.maximum(i * per - 1, 0), c))
    nxt = lambda c: (lambda b, r, i: (b, r, jnp.minimum((i + 1) * per, n // side - 1), c))
    blk = lambda rows, imap: pl.BlockSpec((None, n_res, rows, D_ATT), imap)
    return pl.pallas_call(
        functools.partial(_dil_attn_kernel, n=n),
        grid=(batch, dil // n_res, nt),
        in_specs=[pl.BlockSpec(bucket.shape, lambda b, r, i: (0, 0)),
                  pl.BlockSpec(memory_space=pltpu.SMEM),
                  blk(tq, main(0)), blk(tq, main(1)), blk(tq, main(2)),
                  blk(side, prev(1)), blk(side, prev(2)),
                  blk(side, nxt(1)), blk(side, nxt(2))],
        out_specs=[blk(tq, main(0)), blk(tq, main(0))],
        out_shape=[jax.ShapeDtypeStruct((batch, dil, n, D_ATT), BF16),
                   jax.ShapeDtypeStruct((batch, dil, n, D_ATT), F32)],
        scratch_shapes=[pltpu.VMEM((n_res, tq + 2 * side, D_ATT), BF16),
                        pltpu.VMEM((n_res, tq + 2 * side, D_ATT), BF16),
                        pltpu.VMEM((N_ATT_HEADS, Q_BLOCK, 2 * Q_BLOCK), F32)],
        compiler_params=_params(3),
        name=f"dil_attn_d{dil}",
    )(bucket, relb_flat, qkv, qkv, qkv, qkv, qkv, qkv, qkv)


def _mem_kv_kernel(m_ref, g_ref, w_ref, bd_ref, kg_ref, kv_ref):
    x = m_ref[...]
    ms = jnp.mean(x * x, axis=-1, keepdims=True)
    h = (x * lax.rsqrt(ms + RMS_EPS) * g_ref[...]).astype(BF16)
    kv = jnp.dot(h, w_ref[...], preferred_element_type=F32)
    k = kv[:, 0:D_MEM]
    kn = k * lax.rsqrt(_head_mean_square(k, bd_ref[...]) + RMS_EPS) * kg_ref[...]
    kv_ref[:, 0:D_MEM] = kn.astype(BF16)
    kv_ref[:, D_MEM:] = kv[:, D_MEM:].astype(BF16)


def _mem_kv(mem, g, w, bd, kg):
    batch, n_mem, d = mem.shape
    const = lambda b: (0, 0)
    return pl.pallas_call(
        _mem_kv_kernel,
        grid=(batch,),
        in_specs=[pl.BlockSpec((None, n_mem, d), lambda b: (b, 0, 0)),
                  pl.BlockSpec((1, d), const),
                  pl.BlockSpec((d, 2 * D_MEM), const),
                  pl.BlockSpec((MXU_EDGE, MXU_EDGE), const),
                  pl.BlockSpec((1, D_MEM), const)],
        out_specs=pl.BlockSpec((None, n_mem, 2 * D_MEM), lambda b: (b, 0, 0)),
        out_shape=jax.ShapeDtypeStruct((batch, n_mem, 2 * D_MEM), BF16),
        compiler_params=_params(1),
        name="mem_kv",
    )(mem, g, w, bd, kg)


def _out_proj_kernel(x_ref, ct_ref, nrm_ref, x0_ref, p_ref, o1_ref, l1_ref, o4_ref, l4_ref,
                     o16_ref, l16_ref, qm_ref, kv_ref, gs_ref, w_ref, y_ref,
                     so4_ref, sl4_ref, so16_ref, sl16_ref, tmp_ref, mixed_ref):
    nbat, t, d = x_ref.shape
    rows = nbat * t

    def gated(value, cols):
        gate = gs_ref[:, :, cols].astype(F32).reshape(rows, cols.stop - cols.start)
        mixed_ref[:, cols] = (value * gate).astype(BF16)

    conv = (ct_ref[...].astype(F32) * nrm_ref[...]).T
    y_hy = (x0_ref[...].astype(F32).reshape(rows, D_HY) * conv
            + p_ref[...].astype(F32).reshape(rows, D_HY))
    gated(y_hy, slice(0, D_HY))

    tq = t // 4
    for c in range(D_ATT // LANES):
        cols = slice(c * LANES, (c + 1) * LANES)
        for bb in range(nbat):
            for k, (src4, src16, dst4, dst16) in enumerate(((o4_ref, o16_ref, so4_ref, so16_ref),
                                                           (l4_ref, l16_ref, sl4_ref, sl16_ref))):
                tmp = tmp_ref.at[(c * nbat + bb) * 2 + k]
                for r4 in range(4):
                    dst4[c, pl.ds(bb * t + r4, tq, stride=4), :] = (
                        src4[bb, r4, :, cols].astype(F32))
                    for a in range(4):
                        tmp[r4, pl.ds(a, tq // 4, stride=4), :] = (
                            src16[bb, 4 * a + r4, :, cols].astype(F32))
                for r4 in range(4):
                    dst16[c, pl.ds(bb * t + r4, tq, stride=4), :] = tmp[r4]
        l1 = l1_ref[:, :, cols].reshape(rows, LANES)
        l2, l3 = sl4_ref[c], sl16_ref[c]
        top = jnp.maximum(jnp.maximum(l1, l2), l3)
        e1, e2, e3 = jnp.exp(l1 - top), jnp.exp(l2 - top), jnp.exp(l3 - top)
        y_att = ((e1 * o1_ref[:, :, cols].astype(F32).reshape(rows, LANES) + e2 * so4_ref[c]
                  + e3 * so16_ref[c]) * (1.0 / (e1 + e2 + e3)))
        gated(y_att, slice(D_HY + c * LANES, D_HY + (c + 1) * LANES))

    lane = lax.broadcasted_iota(jnp.int32, (t, LANES), 1)
    low_head = lane < HEAD_DIM
    head_sel = [low_head[0:1, :].astype(BF16), jnp.logical_not(low_head)[0:1, :].astype(BF16)]
    for hp in range(N_MEM_HEADS // 2):
        cols = slice(hp * LANES, (hp + 1) * LANES)
        mem_rows = []
        for bb in range(nbat):
            q2 = qm_ref[bb, :, cols]
            k2 = kv_ref[bb, :, cols]
            v2 = kv_ref[bb, :, D_MEM + hp * LANES:D_MEM + (hp + 1) * LANES]
            v2e = jnp.concatenate([v2, jnp.ones_like(v2)], axis=1)
            outs = []
            for hh in range(2):
                qh = q2 * head_sel[hh]
                s = lax.dot_general(qh, k2, (((1,), (1,)), ((), ())),
                                    preferred_element_type=F32)
                mx = jnp.max(s, axis=-1, keepdims=True)
                p = jnp.exp(s - mx).astype(BF16)
                od = jnp.dot(p, v2e, preferred_element_type=F32)
                outs.append(od[:, 0:LANES] * (1.0 / od[:, LANES:]))
            mem_rows.append(jnp.where(low_head, outs[0], outs[1]))
        base = D_HY + D_ATT + hp * LANES
        gated(jnp.concatenate(mem_rows, axis=0), slice(base, base + LANES))

    y = x_ref[...].reshape(rows, d) + jnp.dot(mixed_ref[...], w_ref[...],
                                              preferred_element_type=F32)
    y_ref[...] = y.reshape(nbat, t, d)


def _out_proj(x, conv_t, nrm, x0, p, att, qm, kv, gs, w):
    batch, seq, d = x.shape
    t = SEQ_BLOCK
    nb = seq // t
    nbat = TILE_BATCH
    n_chunks = D_ATT // LANES
    const = lambda b, j: (0, 0)
    tok = lambda width: pl.BlockSpec((nbat, t, width), lambda b, j: (b, j, 0))
    res = lambda dil: pl.BlockSpec((nbat, dil, t // dil, D_ATT), lambda b, j: (b, 0, j, 0))
    nat = pl.BlockSpec((nbat, None, t, D_ATT), lambda b, j: (b, 0, j, 0))
    (o1, l1), (o4, l4), (o16, l16) = att
    natural = pltpu.VMEM((n_chunks, nbat * t, LANES), F32)
    return pl.pallas_call(
        _out_proj_kernel,
        grid=(batch // nbat, nb),
        in_specs=[tok(d),
                  pl.BlockSpec((D_HY, nbat * t), lambda b, j: (0, j * (batch // nbat) + b)),
                  pl.BlockSpec((D_HY, 1), const),
                  tok(D_HY), tok(D_HY), nat, nat, res(4), res(4), res(16), res(16),
                  tok(D_MEM),
                  pl.BlockSpec((nbat, N_MEM, 2 * D_MEM), lambda b, j: (b, 0, 0)),
                  tok(D_MIX),
                  pl.BlockSpec((D_MIX, d), const)],
        out_specs=tok(d),
        out_shape=jax.ShapeDtypeStruct((batch, seq, d), F32),
        scratch_shapes=[natural, natural, natural, natural,
                        pltpu.VMEM((n_chunks * nbat * 2, 4, t // 4, LANES), F32),
                        pltpu.VMEM((nbat * t, D_MIX), BF16)],
        compiler_params=_params(2),
        name="out_proj",
    )(x, conv_t, nrm, x0, p, o1, l1, o4, l4, o16, l16, qm, kv, gs, w)


def _encoder_layer(x, mem, norm_in, w_in, hy_conv_w, hy_conv_b, hy_filt_w1, hy_filt_b1,
                   hy_filt_freq, hy_filt_w2, hy_filt_b2, hy_filt_w3, hy_skip, att_q_norm,
                   att_k_norm, mem_norm, w_mem_kv, mem_q_norm, mem_k_norm, w_out, rel_bias):
    batch, seq, d = x.shape
    assert seq % (SEQ_BLOCK * 2) == 0 and batch % TILE_BATCH == 0 and batch % SUBLANES == 0
    assert all(seq // dil >= Q_BLOCK for _, dil in DILATED_CONFIGS)
    assert all(win // (2 * dil) == ATT_SIDE for win, dil in DILATED_CONFIGS)
    assert tuple(dil for _, dil in DILATED_CONFIGS) == (1, 4, 16)
    nb = seq // SEQ_BLOCK
    scale = HEAD_DIM ** -0.5

    blk = np.arange(MXU_EDGE) // HEAD_DIM
    bd = jnp.asarray((blk[:, None] == blk[None, :]).astype(np.float32) / HEAD_DIM, BF16)
    qkg = jnp.concatenate([jnp.tile(att_q_norm.astype(F32), N_ATT_HEADS) * scale,
                           jnp.tile(att_k_norm.astype(F32), N_ATT_HEADS)])[None, :]
    mqg = (jnp.tile(mem_q_norm.astype(F32), N_MEM_HEADS) * scale)[None, :]
    mkg = jnp.tile(mem_k_norm.astype(F32), N_MEM_HEADS)[None, :]

    st, x0, p, q1, q4, q16, qm, gs = _in_proj(
        x, norm_in[None, :], w_in.astype(BF16), bd, qkg, mqg, hy_conv_w.astype(F32),
        hy_conv_b.astype(F32)[None, :], hy_skip.astype(F32)[None, :])

    band = jnp.linspace(1e-4, FILTER_BANDS - 1, FILTER_BANDS, dtype=F32)[:, None]
    deltas = jnp.abs(jnp.linspace(math.log(DECAY_TARGET) / SLOW_DECAY_PCT,
                                  math.log(DECAY_TARGET) / FAST_DECAY_PCT, D_HY, dtype=F32))
    w3t = hy_filt_w3.astype(F32).T
    g, nrm = _hyena_filter(seq, band, hy_filt_w1.astype(F32), hy_filt_b1.astype(F32)[:, None],
                           hy_filt_freq.astype(F32)[:, None], hy_filt_w2.astype(F32).T,
                           hy_filt_b2.astype(F32)[:, None], w3t[0:D_HY], w3t[D_HY:],
                           deltas[:, None])

    conv_t = _hyena_conv(st.reshape(D_HY, nb * batch, SEQ_BLOCK),
                         g.reshape(D_HY, 1, 2 * seq), batch, nb)
    conv_t = conv_t.reshape(D_HY, nb * batch * SEQ_BLOCK)

    relb_flat = rel_bias.astype(F32).reshape(-1)
    att = [_dil_attn(q1.reshape(batch, 1, seq, 3 * D_ATT), relb_flat, 1),
           _dil_attn(q4, relb_flat, 4),
           _dil_attn(q16, relb_flat, 16)]

    kv = _mem_kv(mem, mem_norm[None, :], w_mem_kv.astype(BF16), bd, mkg)

    return _out_proj(x, conv_t, nrm, x0, p, att, qm, kv, gs, w_out.astype(BF16))


def kernel(x_prompt, x_sample, mem_prompt, mem_sample, norm_in, w_in, hy_conv_w, hy_conv_b, hy_filt_w1, hy_filt_b1, hy_filt_freq, hy_filt_w2, hy_filt_b2, hy_filt_w3, hy_skip, att_q_norm, att_k_norm, mem_norm, w_mem_kv, mem_q_norm, mem_k_norm, w_out, rel_bias):
    def trunk(x, mem):
        for l in range(norm_in.shape[0]):
            x = _encoder_layer(x, mem, norm_in[l], w_in[l], hy_conv_w[l], hy_conv_b[l],
                               hy_filt_w1[l], hy_filt_b1[l], hy_filt_freq[l], hy_filt_w2[l],
                               hy_filt_b2[l], hy_filt_w3[l], hy_skip[l], att_q_norm[l],
                               att_k_norm[l], mem_norm[l], w_mem_kv[l], mem_q_norm[l],
                               mem_k_norm[l], w_out[l], rel_bias)
        return x

    return (trunk(x_prompt, mem_prompt), trunk(x_sample, mem_sample))
```

```python
import functools
import math

import numpy as np
import jax
import jax.numpy as jnp
from jax import lax
from jax.experimental import pallas as pl
from jax.experimental.pallas import tpu as pltpu

F32 = jnp.float32
BF16 = jnp.bfloat16

N_MEM = 256
HEAD_DIM = 64
D_HY = 384
N_ATT_HEADS = 6
D_ATT = N_ATT_HEADS * HEAD_DIM
N_MEM_HEADS = 4
D_MEM = N_MEM_HEADS * HEAD_DIM
D_MIX = D_HY + D_ATT + D_MEM
FILTER_BANDS = 16
FILTER_HIDDEN = 64
DECAY_TARGET = 1e-2
FAST_DECAY_PCT = 0.3
SLOW_DECAY_PCT = 1.5
DILATED_CONFIGS = ((128, 1), (512, 4), (2048, 16))
N_BUCKETS = 32
MAX_DISTANCE = 1024
RMS_EPS = 1e-6
NEG_INF = -1e30

SUBLANES = 8
LANES = 128
MXU_EDGE = 256

SEQ_BLOCK = MXU_EDGE
TILE_BATCH = 2
Q_BLOCK = 128
ATT_SIDE = 64
ATT_ROWS = 1024
FILTER_TILE = 2048
ROLL_CHUNK = 1024
CONV_CHANNELS = 16
CONV_GROUP = 4
VMEM_LIMIT = 56 * 1024 * 1024


def _params(n_axes):
    return pltpu.CompilerParams(dimension_semantics=("arbitrary",) * n_axes,
                                vmem_limit_bytes=VMEM_LIMIT)


def _head_mean_square(z, bd):
    return jnp.dot((z * z).astype(BF16), bd, preferred_element_type=F32)


def _in_proj_kernel(x_ref, xp_ref, xn_ref, g_ref, w_ref, bd_ref, qkg_ref, mqg_ref,
                    cw_ref, cb_ref, skip_ref,
                    st_ref, x0_ref, p_ref, q1_ref, q4_ref, q16_ref, qm_ref, gs_ref,
                    h_ref, stage_ref, res4_ref, *, n_blocks):
    i0 = 3 * D_HY
    i1 = i0 + 3 * D_ATT
    i2 = i1 + D_MEM
    j = pl.program_id(1)
    nbat, t, d = x_ref.shape
    halo = xp_ref.shape[1]
    rows = nbat * t

    def normed(xv):
        ms = jnp.mean(xv * xv, axis=-1, keepdims=True)
        return (xv * lax.rsqrt(ms + RMS_EPS) * g_ref[...]).astype(BF16)

    h_ref[0:rows, :] = normed(x_ref[...].reshape(rows, d))
    h_ref[rows:rows + nbat * halo, :] = normed(xp_ref[...].reshape(nbat * halo, d))
    h_ref[rows + nbat * halo:, :] = normed(xn_ref[...].reshape(nbat * halo, d))
    hm = h_ref[0:rows, :]
    bd = bd_ref[...]

    zall = jnp.dot(h_ref[...], w_ref[:, 0:i0], preferred_element_type=F32)
    za = jnp.dot(hm, w_ref[:, i0:i1], preferred_element_type=F32)

    not_first = (j > 0).astype(F32)
    not_last = (j < n_blocks - 1).astype(F32)
    row = lax.broadcasted_iota(jnp.int32, (t, i0), 0)
    for bb in range(nbat):
        z = zall[bb * t:(bb + 1) * t]
        r_before = rows + bb * halo + halo - 1
        r_after = rows + nbat * halo + bb * halo
        z_before = zall[r_before:r_before + 1] * not_first
        z_after = zall[r_after:r_after + 1] * not_last
        zm1 = jnp.where(row == 0, z_before, pltpu.roll(z, 1, axis=0))
        zp1 = jnp.where(row == t - 1, z_after, pltpu.roll(z, t - 1, axis=0))
        u = zm1 * cw_ref[0:1, :] + z * cw_ref[1:2, :] + zp1 * cw_ref[2:3, :] + cb_ref[...]
        x0 = u[:, 0:D_HY]
        s = u[:, D_HY:2 * D_HY] * u[:, 2 * D_HY:]
        x0_ref[bb] = x0.astype(BF16)
        p_ref[bb] = (x0 * (s * skip_ref[...])).astype(BF16)
        st_ref[:, bb * t:(bb + 1) * t] = s.T.astype(BF16)

    gate = jnp.dot(hm, w_ref[:, i2:], preferred_element_type=F32)

    per_edge = MXU_EDGE // LANES
    for c in range(2 * D_ATT // MXU_EDGE):
        sl = slice(c * MXU_EDGE, (c + 1) * MXU_EDGE)
        z = za[:, sl]
        zn = z * lax.rsqrt(_head_mean_square(z, bd) + RMS_EPS) * qkg_ref[:, sl]
        for k in range(per_edge):
            stage_ref[c * per_edge + k] = zn[:, k * LANES:(k + 1) * LANES]
    for c in range(2 * D_ATT // LANES, 3 * D_ATT // LANES):
        stage_ref[c] = za[:, c * LANES:(c + 1) * LANES]

    zq = jnp.dot(hm, w_ref[:, i1:i2], preferred_element_type=F32)

    half_gate = 0.5 * gate
    silu = half_gate * jnp.tanh(half_gate) + half_gate
    gs_ref[...] = silu.astype(BF16).reshape(nbat, t, D_MIX)

    tq = t // 4
    for c in range(3 * D_ATT // LANES):
        cols = slice(c * LANES, (c + 1) * LANES)
        for bb in range(nbat):
            q1_ref[bb, :, cols] = stage_ref[c, bb * t:(bb + 1) * t, :].astype(BF16)
            tmp = res4_ref.at[c * nbat + bb]
            for r4 in range(4):
                sub = stage_ref[c, pl.ds(bb * t + r4, tq, stride=4), :]
                q4_ref[bb, r4, :, cols] = sub.astype(BF16)
                tmp[r4] = sub
            for r4 in range(4):
                for a in range(4):
                    sub = tmp[r4, pl.ds(a, tq // 4, stride=4), :]
                    q16_ref[bb, 4 * a + r4, :, cols] = sub.astype(BF16)

    zq = zq * lax.rsqrt(_head_mean_square(zq, bd) + RMS_EPS) * mqg_ref[...]
    qm_ref[...] = zq.astype(BF16).reshape(nbat, t, D_MEM)


def _in_proj(x, g, w, bd, qkg, mqg, cw, cb, skip):
    batch, seq, d = x.shape
    d_in = w.shape[1]
    t = SEQ_BLOCK
    nb = seq // t
    nbat = TILE_BATCH
    halo = SUBLANES
    per = t // halo
    n_chunks = 3 * D_ATT // LANES
    const = lambda b, j: (0, 0)
    tok = lambda width: pl.BlockSpec((nbat, t, width), lambda b, j: (b, j, 0))
    res = lambda dil: pl.BlockSpec((nbat, dil, t // dil, 3 * D_ATT), lambda b, j: (b, 0, j, 0))
    return pl.pallas_call(
        functools.partial(_in_proj_kernel, n_blocks=nb),
        grid=(batch // nbat, nb),
        in_specs=[tok(d),
                  pl.BlockSpec((nbat, halo, d), lambda b, j: (b, jnp.maximum(j * per - 1, 0), 0)),
                  pl.BlockSpec((nbat, halo, d),
                               lambda b, j: (b, jnp.minimum((j + 1) * per, seq // halo - 1), 0)),
                  pl.BlockSpec((1, d), const),
                  pl.BlockSpec((d, d_in), const),
                  pl.BlockSpec((MXU_EDGE, MXU_EDGE), const),
                  pl.BlockSpec((1, 2 * D_ATT), const),
                  pl.BlockSpec((1, D_MEM), const),
                  pl.BlockSpec((3, 3 * D_HY), const),
                  pl.BlockSpec((1, 3 * D_HY), const),
                  pl.BlockSpec((1, D_HY), const)],
        out_specs=[pl.BlockSpec((D_HY, nbat * t), lambda b, j: (0, j * (batch // nbat) + b)),
                   tok(D_HY), tok(D_HY), tok(3 * D_ATT), res(4), res(16), tok(D_MEM), tok(D_MIX)],
        out_shape=[jax.ShapeDtypeStruct((D_HY, nb * batch * t), BF16),
                   jax.ShapeDtypeStruct((batch, seq, D_HY), BF16),
                   jax.ShapeDtypeStruct((batch, seq, D_HY), BF16),
                   jax.ShapeDtypeStruct((batch, seq, 3 * D_ATT), BF16),
                   jax.ShapeDtypeStruct((batch, 4, seq // 4, 3 * D_ATT), BF16),
                   jax.ShapeDtypeStruct((batch, 16, seq // 16, 3 * D_ATT), BF16),
                   jax.ShapeDtypeStruct((batch, seq, D_MEM), BF16),
                   jax.ShapeDtypeStruct((batch, seq, D_MIX), BF16)],
        scratch_shapes=[pltpu.VMEM((nbat * (t + 2 * halo), d), BF16),
                        pltpu.VMEM((n_chunks, nbat * t, LANES), F32),
                        pltpu.VMEM((n_chunks * nbat, 4, t // 4, LANES), F32)],
        compiler_params=_params(2),
        name="in_proj",
    )(x, x, x, g, w, bd, qkg, mqg, cw, cb, skip)


def _hyena_filter_kernel(band_ref, w1t_ref, w1c_ref, w1s_ref, b1_ref, fr_ref, w2t_ref, b2_ref,
                         w3f_ref, w3b_ref, dl_ref, g_ref, nrm_ref, ss_ref, *, seq, n_tiles):
    i = pl.program_id(0)
    tl = g_ref.shape[1]
    hi = lax.Precision.HIGHEST
    m = i * tl + lax.broadcasted_iota(jnp.int32, (1, tl), 1)
    lag = m - seq
    pos = jnp.abs(lag).astype(F32)
    tt = pos / float(max(seq - 1, 1))
    ang = ((2.0 * math.pi / seq) * pos) * band_ref[...]
    pre = (w1t_ref[...] * tt
           + jnp.dot(w1c_ref[...], jnp.cos(ang), precision=hi, preferred_element_type=F32)
           + jnp.dot(w1s_ref[...], -jnp.sin(ang), precision=hi, preferred_element_type=F32))
    fr = fr_ref[...]
    h = jnp.sin(fr * (pre + b1_ref[...]))
    h = jnp.sin(fr * (jnp.dot(w2t_ref[...], h, precision=hi, preferred_element_type=F32)
                      + b2_ref[...]))
    fwd = jnp.dot(w3f_ref[...], h, precision=hi, preferred_element_type=F32)
    bwd = jnp.dot(w3b_ref[...], h, precision=hi, preferred_element_type=F32)
    decay = jnp.exp(-tt * dl_ref[...])
    val = jnp.where(lag > 0, fwd, jnp.where(lag < 0, bwd, fwd + bwd)) * decay
    val = jnp.where(m == 0, 0.0, val)
    g_ref[...] = val

    @pl.when(i == 0)
    def _():
        ss_ref[...] = jnp.zeros_like(ss_ref)

    ss_ref[...] += jnp.sum(val * val, axis=1, keepdims=True)

    @pl.when(i == n_tiles - 1)
    def _():
        nrm_ref[...] = lax.rsqrt(ss_ref[...] + 1e-12)


def _hyena_filter(seq, band, w1, b1, fr, w2t, b2, w3f, w3b, deltas):
    tl = min(FILTER_TILE, 2 * seq)
    n_tiles = 2 * seq // tl
    const = lambda i: (0, 0)
    full = lambda a: pl.BlockSpec(a.shape, const)
    w1t = w1.T
    args = (band, w1t[:, 0:1], w1t[:, 1:1 + FILTER_BANDS], w1t[:, 1 + FILTER_BANDS:], b1, fr,
            w2t, b2, w3f, w3b, deltas)
    return pl.pallas_call(
        functools.partial(_hyena_filter_kernel, seq=seq, n_tiles=n_tiles),
        grid=(n_tiles,),
        in_specs=[full(a) for a in args],
        out_specs=[pl.BlockSpec((D_HY, tl), lambda i: (0, i)),
                   pl.BlockSpec((D_HY, 1), const)],
        out_shape=[jax.ShapeDtypeStruct((D_HY, 2 * seq), F32),
                   jax.ShapeDtypeStruct((D_HY, 1), F32)],
        scratch_shapes=[pltpu.VMEM((D_HY, 1), F32)],
        compiler_params=_params(1),
        name="hyena_filter",
    )(*args)


def _expand_filter(g_ref, e_ref, nb):
    t = SEQ_BLOCK
    width = 2 * nb * t
    half = t // 2
    chunk = min(ROLL_CHUNK, width)
    for c in range(width // chunk):
        m0 = c * chunk
        lo = max(m0 - half, 0)
        win = g_ref[:, lo:m0 + chunk]
        rolled = pltpu.roll(jnp.broadcast_to(win, (half, win.shape[1])), 0, axis=1,
                            stride=1, stride_axis=0)
        blk = rolled[:, m0 - lo:].astype(BF16)
        for w in range(chunk // half):
            piece = blk[:, w * half:(w + 1) * half]
            m_up = m0 + w * half
            r_up = (m_up // t) * t
            e_ref[r_up:r_up + half, m_up % t:m_up % t + half] = piece
            m_lo = m_up + half
            if m_lo < width:
                r_lo = (m_lo // t) * t + half
                e_ref[r_lo:r_lo + half, m_lo % t:m_lo % t + half] = piece


def _toeplitz_matmuls(a_ref, e_ref, o_ref, ak_ref, sh_ref, acc_ref, batch, nb):
    t = SEQ_BLOCK
    k = CONV_GROUP
    rows_a = batch * nb
    sh_ref[0:batch, :] = jnp.zeros((batch, t), F32)
    sh_ref[batch:batch + rows_a, :] = a_ref[...].astype(F32)
    sh_ref[batch + rows_a:, :] = jnp.zeros((batch, t), F32)
    a_shift = sh_ref[...].astype(BF16)
    ak_ref[...] = jnp.zeros_like(ak_ref)
    for e in range(k):
        if e % 2 == 0:
            ak_ref[e * batch:e * batch + rows_a, e * t:(e + 1) * t] = a_ref[...]
        else:
            r0 = (e - 1) * batch
            ak_ref[r0:r0 + rows_a + 2 * batch, e * t:(e + 1) * t] = a_shift
    acc_ref[...] = jnp.zeros_like(acc_ref)
    for q0 in range(0, 2 * nb, k):
        e_lo = 1 if q0 == 0 else 0
        d0 = q0 - nb
        j_lo = max(-d0, e_lo)
        j_hi = min(nb - d0, nb + k - 1)
        j_lo -= j_lo % 2
        j_hi += j_hi % 2
        r_out = batch + batch * (j_lo + d0)
        acc_ref[r_out:r_out + batch * (j_hi - j_lo), :] += jnp.dot(
            ak_ref[batch * j_lo:batch * j_hi, e_lo * t:k * t],
            e_ref[(q0 + e_lo) * t:(q0 + k) * t, :], preferred_element_type=F32)
    o_ref[...] = acc_ref[batch:batch + batch * nb, :].astype(o_ref.dtype)


def _hyena_conv_kernel(a_ref, g_ref, o_ref, e0_ref, e1_ref, ak_ref, sh_ref, acc_ref, *,
                       batch, nb):
    cg = a_ref.shape[0]
    _expand_filter(g_ref.at[0], e0_ref, nb)

    def body(kk, carry):
        c0 = 2 * kk
        _expand_filter(g_ref.at[c0 + 1], e1_ref, nb)
        _toeplitz_matmuls(a_ref.at[c0], e0_ref, o_ref.at[c0], ak_ref, sh_ref, acc_ref, batch, nb)
        _expand_filter(g_ref.at[jnp.minimum(c0 + 2, cg - 1)], e0_ref, nb)
        _toeplitz_matmuls(a_ref.at[c0 + 1], e1_ref, o_ref.at[c0 + 1], ak_ref, sh_ref, acc_ref,
                          batch, nb)
        return carry

    lax.fori_loop(0, cg // 2, body, 0)


def _hyena_conv(st3, g3, batch, nb):
    t = SEQ_BLOCK
    rows = nb * batch
    cg = CONV_CHANNELS
    k = CONV_GROUP
    grp = lambda c: (c, 0, 0)
    expand = pltpu.VMEM((2 * nb * t, t), BF16)
    return pl.pallas_call(
        functools.partial(_hyena_conv_kernel, batch=batch, nb=nb),
        grid=(D_HY // cg,),
        in_specs=[pl.BlockSpec((cg, rows, t), grp),
                  pl.BlockSpec((cg, 1, 2 * nb * t), grp)],
        out_specs=pl.BlockSpec((cg, rows, t), grp),
        out_shape=jax.ShapeDtypeStruct((D_HY, rows, t), BF16),
        scratch_shapes=[expand, expand,
                        pltpu.VMEM((rows + k * batch, k * t), BF16),
                        pltpu.VMEM((rows + 2 * batch, t), F32),
                        pltpu.VMEM((rows + 2 * batch, t), F32)],
        compiler_params=_params(1),
        name="hyena_conv",
    )(st3, g3)


def _dil_attn_kernel(bucket_ref, relb_ref, q_ref, k_ref, v_ref, kp_ref, vp_ref, kn_ref, vn_ref,
                     o_ref, lse_ref, kall_ref, vall_ref, bias_ref, *, n):
    n_res, tq, _ = q_ref.shape
    side = ATT_SIDE
    first = (pl.program_id(0) == 0) & (pl.program_id(1) == 0) & (pl.program_id(2) == 0)

    @pl.when(first)
    def _():
        bucket = bucket_ref[...]
        for h in range(N_ATT_HEADS):
            bias_ref[h] = jnp.full(bucket.shape, NEG_INF, F32)

        def body(bk, carry):
            hit = bucket == bk
            for h in range(N_ATT_HEADS):
                bias_ref[h] = jnp.where(hit, relb_ref[bk * N_ATT_HEADS + h], bias_ref[h])
            return carry

        lax.fori_loop(0, N_BUCKETS, body, 0)

    i = pl.program_id(2)
    lane = lax.broadcasted_iota(jnp.int32, (Q_BLOCK, LANES), 1)
    low_head = lane < HEAD_DIM
    head_sel = [low_head[0:1, :].astype(BF16), jnp.logical_not(low_head)[0:1, :].astype(BF16)]
    n_sub = tq // Q_BLOCK
    for rr in range(n_res):
        kall_ref[rr, 0:side, :] = kp_ref[rr]
        kall_ref[rr, side:side + tq, :] = k_ref[rr]
        kall_ref[rr, side + tq:, :] = kn_ref[rr]
        vall_ref[rr, 0:side, :] = vp_ref[rr]
        vall_ref[rr, side:side + tq, :] = v_ref[rr]
        vall_ref[rr, side + tq:, :] = vn_ref[rr]
        for j in range(n_sub):
            r0 = j * Q_BLOCK
            at_edge = j == 0 or j == n_sub - 1
            if at_edge:
                kpos = (i * tq + r0 - side
                        + lax.broadcasted_iota(jnp.int32, (1, 2 * Q_BLOCK), 1))
                in_seq = (kpos >= 0) & (kpos < n)
            lse_rows = jnp.zeros((Q_BLOCK, LANES), F32)
            for hp in range(N_ATT_HEADS // 2):
                cols = slice(hp * LANES, (hp + 1) * LANES)
                q2 = q_ref[rr, r0:r0 + Q_BLOCK, cols]
                k2 = kall_ref[rr, r0:r0 + 2 * Q_BLOCK, cols]
                v2 = vall_ref[rr, r0:r0 + 2 * Q_BLOCK, cols]
                v2e = jnp.concatenate([v2, jnp.ones_like(v2)], axis=1)
                outs = []
                for hh in range(2):
                    qh = q2 * head_sel[hh]
                    s = lax.dot_general(qh, k2, (((1,), (1,)), ((), ())),
                                        preferred_element_type=F32)
                    s = s + bias_ref[2 * hp + hh]
                    if at_edge:
                        s = jnp.where(in_seq, s, NEG_INF)
                    mx = jnp.max(s, axis=-1, keepdims=True)
                    p = jnp.exp(s - mx).astype(BF16)
                    od = jnp.dot(p, v2e, preferred_element_type=F32)
                    den = od[:, LANES:]
                    outs.append(od[:, 0:LANES] * (1.0 / den))
                    lse_rows = jnp.where(lane == 2 * hp + hh, mx + jnp.log(den), lse_rows)
                o_ref[rr, r0:r0 + Q_BLOCK, cols] = jnp.where(
                    low_head, outs[0], outs[1]).astype(BF16)
            lse_ref[rr, r0:r0 + Q_BLOCK, :] = lse_rows


def _attn_bucket_table(dil):
    half = N_BUCKETS // 2
    max_exact = half // 2
    qi = np.arange(Q_BLOCK)[:, None]
    kj = np.arange(2 * Q_BLOCK)[None, :]
    rel_steps = kj - ATT_SIDE - qi
    rel = rel_steps * dil
    ret = np.where(rel > 0, half, 0)
    na = np.abs(rel)
    large = max_exact + (np.log(np.maximum(na, 1).astype(np.float32) / np.float32(max_exact))
                         / np.float32(math.log(MAX_DISTANCE / max_exact))
                         * np.float32(half - max_exact)).astype(np.int32)
    large = np.minimum(large, half - 1)
    bucket = ret + np.where(na < max_exact, na, large)
    return np.where(np.abs(rel_steps) <= ATT_SIDE, bucket, -1).astype(np.int32)


def _dil_attn(qkv, relb_flat, dil):
    batch, _, n, _ = qkv.shape
    tq = min(n, ATT_ROWS)
    nt = n // tq
    n_res = min(dil, ATT_ROWS // tq)
    side = ATT_SIDE
    per = tq // side
    bucket = jnp.asarray(_attn_bucket_table(dil))
    main = lambda c: (lambda b, r, i: (b, r, i, c))
    prev = lambda c: (lambda b, r, i: (b, r, jnp.maximum(i * per - 1, 0), c))
    nxt = lambda c: (lambda b, r, i: (b, r, jnp.minimum((i + 1) * per, n // side - 1), c))
    blk = lambda rows, imap: pl.BlockSpec((None, n_res, rows, D_ATT), imap)
    return pl.pallas_call(
        functools.partial(_dil_attn_kernel, n=n),
        grid=(batch, dil // n_res, nt),
        in_specs=[pl.BlockSpec(bucket.shape, lambda b, r, i: (0, 0)),
                  pl.BlockSpec(memory_space=pltpu.SMEM),
                  blk(tq, main(0)), blk(tq, main(1)), blk(tq, main(2)),
                  blk(side, prev(1)), blk(side, prev(2)),
                  blk(side, nxt(1)), blk(side, nxt(2))],
        out_specs=[blk(tq, main(0)), pl.BlockSpec((None, n_res, tq, LANES), main(0))],
        out_shape=[jax.ShapeDtypeStruct((batch, dil, n, D_ATT), BF16),
                   jax.ShapeDtypeStruct((batch, dil, n, LANES), F32)],
        scratch_shapes=[pltpu.VMEM((n_res, tq + 2 * side, D_ATT), BF16),
                        pltpu.VMEM((n_res, tq + 2 * side, D_ATT), BF16),
                        pltpu.VMEM((N_ATT_HEADS, Q_BLOCK, 2 * Q_BLOCK), F32)],
        compiler_params=_params(3),
        name=f"dil_attn_d{dil}",
    )(bucket, relb_flat, qkv, qkv, qkv, qkv, qkv, qkv, qkv)


def _mem_kv_kernel(m_ref, g_ref, w_ref, bd_ref, kg_ref, kv_ref):
    x = m_ref[...]
    ms = jnp.mean(x * x, axis=-1, keepdims=True)
    h = (x * lax.rsqrt(ms + RMS_EPS) * g_ref[...]).astype(BF16)
    kv = jnp.dot(h, w_ref[...], preferred_element_type=F32)
    k = kv[:, 0:D_MEM]
    kn = k * lax.rsqrt(_head_mean_square(k, bd_ref[...]) + RMS_EPS) * kg_ref[...]
    kv_ref[:, 0:D_MEM] = kn.astype(BF16)
    kv_ref[:, D_MEM:] = kv[:, D_MEM:].astype(BF16)


def _mem_kv(mem, g, w, bd, kg):
    batch, n_mem, d = mem.shape
    const = lambda b: (0, 0)
    return pl.pallas_call(
        _mem_kv_kernel,
        grid=(batch,),
        in_specs=[pl.BlockSpec((None, n_mem, d), lambda b: (b, 0, 0)),
                  pl.BlockSpec((1, d), const),
                  pl.BlockSpec((d, 2 * D_MEM), const),
                  pl.BlockSpec((MXU_EDGE, MXU_EDGE), const),
                  pl.BlockSpec((1, D_MEM), const)],
        out_specs=pl.BlockSpec((None, n_mem, 2 * D_MEM), lambda b: (b, 0, 0)),
        out_shape=jax.ShapeDtypeStruct((batch, n_mem, 2 * D_MEM), BF16),
        compiler_params=_params(1),
        name="mem_kv",
    )(mem, g, w, bd, kg)


def _out_proj_kernel(x_ref, ct_ref, nrm_ref, x0_ref, p_ref, o1_ref, l1_ref, o4_ref, l4_ref,
                     o16_ref, l16_ref, sel_ref, qm_ref, kv_ref, gs_ref, w_ref, y_ref,
                     so4_ref, sl4_ref, so16_ref, sl16_ref, tmp_ref, mixed_ref):
    nbat, t, d = x_ref.shape
    rows = nbat * t

    def gated(value, cols):
        gate = gs_ref[:, :, cols].astype(F32).reshape(rows, cols.stop - cols.start)
        mixed_ref[:, cols] = (value * gate).astype(BF16)

    conv = (ct_ref[...].astype(F32) * nrm_ref[...]).T
    y_hy = (x0_ref[...].astype(F32).reshape(rows, D_HY) * conv
            + p_ref[...].astype(F32).reshape(rows, D_HY))
    gated(y_hy, slice(0, D_HY))

    tq = t // 4

    def to_natural(src4, src16, dst4, dst16, c, cols, slot):
        for bb in range(nbat):
            tmp = tmp_ref.at[slot * nbat + bb]
            for r4 in range(4):
                dst4[c, pl.ds(bb * t + r4, tq, stride=4), :] = src4[bb, r4, :, cols].astype(F32)
                for a in range(4):
                    tmp[r4, pl.ds(a, tq // 4, stride=4), :] = (
                        src16[bb, 4 * a + r4, :, cols].astype(F32))
            for r4 in range(4):
                dst16[c, pl.ds(bb * t + r4, tq, stride=4), :] = tmp[r4]

    n_chunks = D_ATT // LANES
    to_natural(l4_ref, l16_ref, sl4_ref, sl16_ref, 0, slice(0, LANES), n_chunks)
    l1 = l1_ref[...].reshape(rows, LANES)
    l2, l3 = sl4_ref[0], sl16_ref[0]
    top = jnp.maximum(jnp.maximum(l1, l2), l3)
    e1, e2, e3 = jnp.exp(l1 - top), jnp.exp(l2 - top), jnp.exp(l3 - top)
    inv = 1.0 / (e1 + e2 + e3)
    spread = [jnp.dot((e * inv).astype(BF16), sel_ref[...], preferred_element_type=F32)
              for e in (e1, e2, e3)]
    for c in range(n_chunks):
        cols = slice(c * LANES, (c + 1) * LANES)
        to_natural(o4_ref, o16_ref, so4_ref, so16_ref, c, cols, c)
        y_att = (spread[0][:, cols] * o1_ref[:, :, cols].astype(F32).reshape(rows, LANES)
                 + spread[1][:, cols] * so4_ref[c] + spread[2][:, cols] * so16_ref[c])
        gated(y_att, slice(D_HY + c * LANES, D_HY + (c + 1) * LANES))

    lane = lax.broadcasted_iota(jnp.int32, (t, LANES), 1)
    low_head = lane < HEAD_DIM
    head_sel = [low_head[0:1, :].astype(BF16), jnp.logical_not(low_head)[0:1, :].astype(BF16)]
    for hp in range(N_MEM_HEADS // 2):
        cols = slice(hp * LANES, (hp + 1) * LANES)
        mem_rows = []
        for bb in range(nbat):
            q2 = qm_ref[bb, :, cols]
            k2 = kv_ref[bb, :, cols]
            v2 = kv_ref[bb, :, D_MEM + hp * LANES:D_MEM + (hp + 1) * LANES]
            v2e = jnp.concatenate([v2, jnp.ones_like(v2)], axis=1)
            outs = []
            for hh in range(2):
                qh = q2 * head_sel[hh]
                s = lax.dot_general(qh, k2, (((1,), (1,)), ((), ())),
                                    preferred_element_type=F32)
                mx = jnp.max(s, axis=-1, keepdims=True)
                p = jnp.exp(s - mx).astype(BF16)
                od = jnp.dot(p, v2e, preferred_element_type=F32)
                outs.append(od[:, 0:LANES] * (1.0 / od[:, LANES:]))
            mem_rows.append(jnp.where(low_head, outs[0], outs[1]))
        base = D_HY + D_ATT + hp * LANES
        gated(jnp.concatenate(mem_rows, axis=0), slice(base, base + LANES))

    y = x_ref[...].reshape(rows, d) + jnp.dot(mixed_ref[...], w_ref[...],
                                              preferred_element_type=F32)
    y_ref[...] = y.reshape(nbat, t, d)


def _out_proj(x, conv_t, nrm, x0, p, att, qm, kv, gs, w):
    batch, seq, d = x.shape
    t = SEQ_BLOCK
    nb = seq // t
    nbat = TILE_BATCH
    n_chunks = D_ATT // LANES
    const = lambda b, j: (0, 0)
    tok = lambda width: pl.BlockSpec((nbat, t, width), lambda b, j: (b, j, 0))
    res = lambda dil, width: pl.BlockSpec((nbat, dil, t // dil, width),
                                          lambda b, j: (b, 0, j, 0))
    nat = lambda width: pl.BlockSpec((nbat, None, t, width), lambda b, j: (b, 0, j, 0))
    (o1, l1), (o4, l4), (o16, l16) = att
    natural = pltpu.VMEM((n_chunks, nbat * t, LANES), F32)
    natural_lse = pltpu.VMEM((1, nbat * t, LANES), F32)
    head_of_lane = np.arange(D_ATT) // HEAD_DIM
    sel = jnp.asarray(np.arange(LANES)[:, None] == head_of_lane[None, :], BF16)
    return pl.pallas_call(
        _out_proj_kernel,
        grid=(batch // nbat, nb),
        in_specs=[tok(d),
                  pl.BlockSpec((D_HY, nbat * t), lambda b, j: (0, j * (batch // nbat) + b)),
                  pl.BlockSpec((D_HY, 1), const),
                  tok(D_HY), tok(D_HY), nat(D_ATT), nat(LANES), res(4, D_ATT), res(4, LANES),
                  res(16, D_ATT), res(16, LANES),
                  pl.BlockSpec((LANES, D_ATT), const),
                  tok(D_MEM),
                  pl.BlockSpec((nbat, N_MEM, 2 * D_MEM), lambda b, j: (b, 0, 0)),
                  tok(D_MIX),
                  pl.BlockSpec((D_MIX, d), const)],
        out_specs=tok(d),
        out_shape=jax.ShapeDtypeStruct((batch, seq, d), F32),
        scratch_shapes=[natural, natural_lse, natural, natural_lse,
                        pltpu.VMEM(((n_chunks + 1) * nbat, 4, t // 4, LANES), F32),
                        pltpu.VMEM((nbat * t, D_MIX), BF16)],
        compiler_params=_params(2),
        name="out_proj",
    )(x, conv_t, nrm, x0, p, o1, l1, o4, l4, o16, l16, sel, qm, kv, gs, w)


def _encoder_layer(x, mem, norm_in, w_in, hy_conv_w, hy_conv_b, hy_filt_w1, hy_filt_b1,
                   hy_filt_freq, hy_filt_w2, hy_filt_b2, hy_filt_w3, hy_skip, att_q_norm,
                   att_k_norm, mem_norm, w_mem_kv, mem_q_norm, mem_k_norm, w_out, rel_bias):
    batch, seq, d = x.shape
    assert seq % (SEQ_BLOCK * 2) == 0 and batch % TILE_BATCH == 0 and batch % SUBLANES == 0
    assert all(seq // dil >= Q_BLOCK for _, dil in DILATED_CONFIGS)
    assert all(win // (2 * dil) == ATT_SIDE for win, dil in DILATED_CONFIGS)
    assert tuple(dil for _, dil in DILATED_CONFIGS) == (1, 4, 16)
    nb = seq // SEQ_BLOCK
    scale = HEAD_DIM ** -0.5

    blk = np.arange(MXU_EDGE) // HEAD_DIM
    bd = jnp.asarray((blk[:, None] == blk[None, :]).astype(np.float32) / HEAD_DIM, BF16)
    qkg = jnp.concatenate([jnp.tile(att_q_norm.astype(F32), N_ATT_HEADS) * scale,
                           jnp.tile(att_k_norm.astype(F32), N_ATT_HEADS)])[None, :]
    mqg = (jnp.tile(mem_q_norm.astype(F32), N_MEM_HEADS) * scale)[None, :]
    mkg = jnp.tile(mem_k_norm.astype(F32), N_MEM_HEADS)[None, :]

    st, x0, p, q1, q4, q16, qm, gs = _in_proj(
        x, norm_in[None, :], w_in.astype(BF16), bd, qkg, mqg, hy_conv_w.astype(F32),
        hy_conv_b.astype(F32)[None, :], hy_skip.astype(F32)[None, :])

    band = jnp.linspace(1e-4, FILTER_BANDS - 1, FILTER_BANDS, dtype=F32)[:, None]
    deltas = jnp.abs(jnp.linspace(math.log(DECAY_TARGET) / SLOW_DECAY_PCT,
                                  math.log(DECAY_TARGET) / FAST_DECAY_PCT, D_HY, dtype=F32))
    w3t = hy_filt_w3.astype(F32).T
    g, nrm = _hyena_filter(seq, band, hy_filt_w1.astype(F32), hy_filt_b1.astype(F32)[:, None],
                           hy_filt_freq.astype(F32)[:, None], hy_filt_w2.astype(F32).T,
                           hy_filt_b2.astype(F32)[:, None], w3t[0:D_HY], w3t[D_HY:],
                           deltas[:, None])

    conv_t = _hyena_conv(st.reshape(D_HY, nb * batch, SEQ_BLOCK),
                         g.reshape(D_HY, 1, 2 * seq), batch, nb)
    conv_t = conv_t.reshape(D_HY, nb * batch * SEQ_BLOCK)

    relb_flat = rel_bias.astype(F32).reshape(-1)
    att = [_dil_attn(q1.reshape(batch, 1, seq, 3 * D_ATT), relb_flat, 1),
           _dil_attn(q4, relb_flat, 4),
           _dil_attn(q16, relb_flat, 16)]

    kv = _mem_kv(mem, mem_norm[None, :], w_mem_kv.astype(BF16), bd, mkg)

    return _out_proj(x, conv_t, nrm, x0, p, att, qm, kv, gs, w_out.astype(BF16))


def kernel(x_prompt, x_sample, mem_prompt, mem_sample, norm_in, w_in, hy_conv_w, hy_conv_b, hy_filt_w1, hy_filt_b1, hy_filt_freq, hy_filt_w2, hy_filt_b2, hy_filt_w3, hy_skip, att_q_norm, att_k_norm, mem_norm, w_mem_kv, mem_q_norm, mem_k_norm, w_out, rel_bias):
    def trunk(x, mem):
        for l in range(norm_in.shape[0]):
            x = _encoder_layer(x, mem, norm_in[l], w_in[l], hy_conv_w[l], hy_conv_b[l],
                               hy_filt_w1[l], hy_filt_b1[l], hy_filt_freq[l], hy_filt_w2[l],
                               hy_filt_b2[l], hy_filt_w3[l], hy_skip[l], att_q_norm[l],
                               att_k_norm[l], mem_norm[l], w_mem_kv[l], mem_q_norm[l],
                               mem_k_norm[l], w_out[l], rel_bias)
        return x

    return (trunk(x_prompt, mem_prompt), trunk(x_sample, mem_sample))
```

```python
import functools
import math

import numpy as np
import jax
import jax.numpy as jnp
from jax import lax
from jax.experimental import pallas as pl
from jax.experimental.pallas import tpu as pltpu

F32 = jnp.float32
BF16 = jnp.bfloat16

N_MEM = 256
HEAD_DIM = 64
D_HY = 384
N_ATT_HEADS = 6
D_ATT = N_ATT_HEADS * HEAD_DIM
N_MEM_HEADS = 4
D_MEM = N_MEM_HEADS * HEAD_DIM
D_MIX = D_HY + D_ATT + D_MEM
FILTER_BANDS = 16
FILTER_HIDDEN = 64
DECAY_TARGET = 1e-2
FAST_DECAY_PCT = 0.3
SLOW_DECAY_PCT = 1.5
DILATED_CONFIGS = ((128, 1), (512, 4), (2048, 16))
N_BUCKETS = 32
MAX_DISTANCE = 1024
RMS_EPS = 1e-6
NEG_INF = -1e30

SUBLANES = 8
LANES = 128
MXU_EDGE = 256

SEQ_BLOCK = MXU_EDGE
TILE_BATCH = 2
Q_BLOCK = 128
ATT_SIDE = 64
ATT_ROWS = 2048
FILTER_TILE = 2048
ROLL_CHUNK = 1024
CONV_CHANNELS = 16
CONV_GROUP = 4
VMEM_LIMIT = 56 * 1024 * 1024


def _params(n_axes):
    return pltpu.CompilerParams(dimension_semantics=("arbitrary",) * n_axes,
                                vmem_limit_bytes=VMEM_LIMIT)


def _head_mean_square(z, bd):
    return jnp.dot((z * z).astype(BF16), bd, preferred_element_type=F32)


def _in_proj_kernel(x_ref, xp_ref, xn_ref, w_ref, bd_ref, qkg_ref, mqg_ref, cw_ref, cb_ref,
                    st_ref, x0_ref, q1_ref, q4_ref, q16_ref, qm_ref, gs_ref,
                    h_ref, stage_ref, res4_ref, *, n_blocks):
    i0 = 3 * D_HY
    i1 = i0 + 3 * D_ATT
    i2 = i1 + D_MEM
    j = pl.program_id(1)
    nbat, t, d = x_ref.shape
    halo = xp_ref.shape[1]
    rows = nbat * t

    def normed(xv):
        ms = jnp.mean(xv * xv, axis=-1, keepdims=True)
        return (xv * lax.rsqrt(ms + RMS_EPS)).astype(BF16)

    h_ref[0:rows, :] = normed(x_ref[...].reshape(rows, d))
    h_ref[rows:rows + nbat * halo, :] = normed(xp_ref[...].reshape(nbat * halo, d))
    h_ref[rows + nbat * halo:, :] = normed(xn_ref[...].reshape(nbat * halo, d))
    hm = h_ref[0:rows, :]
    bd = bd_ref[...]

    zall = jnp.dot(h_ref[...], w_ref[:, 0:i0], preferred_element_type=F32)
    za = jnp.dot(hm, w_ref[:, i0:i1], preferred_element_type=F32)

    not_first = (j > 0).astype(F32)
    not_last = (j < n_blocks - 1).astype(F32)
    row = lax.broadcasted_iota(jnp.int32, (t, i0), 0)
    for bb in range(nbat):
        z = zall[bb * t:(bb + 1) * t]
        r_before = rows + bb * halo + halo - 1
        r_after = rows + nbat * halo + bb * halo
        z_before = zall[r_before:r_before + 1] * not_first
        z_after = zall[r_after:r_after + 1] * not_last
        zm1 = jnp.where(row == 0, z_before, pltpu.roll(z, 1, axis=0))
        zp1 = jnp.where(row == t - 1, z_after, pltpu.roll(z, t - 1, axis=0))
        u = zm1 * cw_ref[0:1, :] + z * cw_ref[1:2, :] + zp1 * cw_ref[2:3, :] + cb_ref[...]
        x0 = u[:, 0:D_HY]
        s = u[:, D_HY:2 * D_HY] * u[:, 2 * D_HY:]
        x0_ref[bb] = x0.astype(BF16)
        st_ref[:, bb * t:(bb + 1) * t] = s.T.astype(BF16)

    gate = jnp.dot(hm, w_ref[:, i2:], preferred_element_type=F32)

    per_edge = MXU_EDGE // LANES
    for c in range(2 * D_ATT // MXU_EDGE):
        sl = slice(c * MXU_EDGE, (c + 1) * MXU_EDGE)
        z = za[:, sl]
        zn = z * lax.rsqrt(_head_mean_square(z, bd) + RMS_EPS) * qkg_ref[:, sl]
        for k in range(per_edge):
            stage_ref[c * per_edge + k] = zn[:, k * LANES:(k + 1) * LANES]
    for c in range(2 * D_ATT // LANES, 3 * D_ATT // LANES):
        stage_ref[c] = za[:, c * LANES:(c + 1) * LANES]

    zq = jnp.dot(hm, w_ref[:, i1:i2], preferred_element_type=F32)

    half_gate = 0.5 * gate
    silu = half_gate * jnp.tanh(half_gate) + half_gate
    gs_ref[...] = silu.astype(BF16).reshape(nbat, t, D_MIX)

    tq = t // 4
    for c in range(3 * D_ATT // LANES):
        cols = slice(c * LANES, (c + 1) * LANES)
        for bb in range(nbat):
            q1_ref[bb, :, cols] = stage_ref[c, bb * t:(bb + 1) * t, :].astype(BF16)
            tmp = res4_ref.at[c * nbat + bb]
            for r4 in range(4):
                sub = stage_ref[c, pl.ds(bb * t + r4, tq, stride=4), :]
                q4_ref[bb, r4, :, cols] = sub.astype(BF16)
                tmp[r4] = sub
            for r4 in range(4):
                for a in range(4):
                    sub = tmp[r4, pl.ds(a, tq // 4, stride=4), :]
                    q16_ref[bb, 4 * a + r4, :, cols] = sub.astype(BF16)

    zq = zq * lax.rsqrt(_head_mean_square(zq, bd) + RMS_EPS) * mqg_ref[...]
    qm_ref[...] = zq.astype(BF16).reshape(nbat, t, D_MEM)


def _in_proj(x, w, bd, qkg, mqg, cw, cb):
    batch, seq, d = x.shape
    d_in = w.shape[1]
    t = SEQ_BLOCK
    nb = seq // t
    nbat = TILE_BATCH
    halo = SUBLANES
    per = t // halo
    n_chunks = 3 * D_ATT // LANES
    const = lambda b, j: (0, 0)
    tok = lambda width: pl.BlockSpec((nbat, t, width), lambda b, j: (b, j, 0))
    res = lambda dil: pl.BlockSpec((nbat, dil, t // dil, 3 * D_ATT), lambda b, j: (b, 0, j, 0))
    return pl.pallas_call(
        functools.partial(_in_proj_kernel, n_blocks=nb),
        grid=(batch // nbat, nb),
        in_specs=[tok(d),
                  pl.BlockSpec((nbat, halo, d), lambda b, j: (b, jnp.maximum(j * per - 1, 0), 0)),
                  pl.BlockSpec((nbat, halo, d),
                               lambda b, j: (b, jnp.minimum((j + 1) * per, seq // halo - 1), 0)),
                  pl.BlockSpec((d, d_in), const),
                  pl.BlockSpec((MXU_EDGE, MXU_EDGE), const),
                  pl.BlockSpec((1, 2 * D_ATT), const),
                  pl.BlockSpec((1, D_MEM), const),
                  pl.BlockSpec((3, 3 * D_HY), const),
                  pl.BlockSpec((1, 3 * D_HY), const)],
        out_specs=[pl.BlockSpec((D_HY, nbat * t), lambda b, j: (0, j * (batch // nbat) + b)),
                   tok(D_HY), tok(3 * D_ATT), res(4), res(16), tok(D_MEM), tok(D_MIX)],
        out_shape=[jax.ShapeDtypeStruct((D_HY, nb * batch * t), BF16),
                   jax.ShapeDtypeStruct((batch, seq, D_HY), BF16),
                   jax.ShapeDtypeStruct((batch, seq, 3 * D_ATT), BF16),
                   jax.ShapeDtypeStruct((batch, 4, seq // 4, 3 * D_ATT), BF16),
                   jax.ShapeDtypeStruct((batch, 16, seq // 16, 3 * D_ATT), BF16),
                   jax.ShapeDtypeStruct((batch, seq, D_MEM), BF16),
                   jax.ShapeDtypeStruct((batch, seq, D_MIX), BF16)],
        scratch_shapes=[pltpu.VMEM((nbat * (t + 2 * halo), d), BF16),
                        pltpu.VMEM((n_chunks, nbat * t, LANES), F32),
                        pltpu.VMEM((n_chunks * nbat, 4, t // 4, LANES), F32)],
        compiler_params=_params(2),
        name="in_proj",
    )(x, x, x, w, bd, qkg, mqg, cw, cb)


def _hyena_filter_kernel(band_ref, w1t_ref, w1c_ref, w1s_ref, b1_ref, fr_ref, w2t_ref, b2_ref,
                         w3f_ref, w3b_ref, dl_ref, skip_ref, g_ref, nrm_ref, tap_ref, ss_ref, *,
                         seq, n_tiles):
    i = pl.program_id(0)
    tl = g_ref.shape[1]
    hi = lax.Precision.HIGHEST
    m = i * tl + lax.broadcasted_iota(jnp.int32, (1, tl), 1)
    lag = m - seq
    pos = jnp.abs(lag).astype(F32)
    tt = pos / float(max(seq - 1, 1))
    ang = ((2.0 * math.pi / seq) * pos) * band_ref[...]
    pre = (w1t_ref[...] * tt
           + jnp.dot(w1c_ref[...], jnp.cos(ang), precision=hi, preferred_element_type=F32)
           + jnp.dot(w1s_ref[...], -jnp.sin(ang), precision=hi, preferred_element_type=F32))
    fr = fr_ref[...]
    h = jnp.sin(fr * (pre + b1_ref[...]))
    h = jnp.sin(fr * (jnp.dot(w2t_ref[...], h, precision=hi, preferred_element_type=F32)
                      + b2_ref[...]))
    fwd = jnp.dot(w3f_ref[...], h, precision=hi, preferred_element_type=F32)
    bwd = jnp.dot(w3b_ref[...], h, precision=hi, preferred_element_type=F32)
    decay = jnp.exp(-tt * dl_ref[...])
    val = jnp.where(lag > 0, fwd, jnp.where(lag < 0, bwd, fwd + bwd)) * decay
    val = jnp.where(m == 0, 0.0, val)
    g_ref[...] = val

    @pl.when(i == 0)
    def _():
        ss_ref[...] = jnp.zeros_like(ss_ref)

    ss_ref[...] += jnp.sum(val * val, axis=1, keepdims=True)

    @pl.when(i == n_tiles - 1)
    def _():
        nrm_ref[...] = lax.rsqrt(ss_ref[...] + 1e-12)
        tap_ref[...] = skip_ref[...] * jnp.sqrt(ss_ref[...] + 1e-12)


def _hyena_filter(seq, band, w1, b1, fr, w2t, b2, w3f, w3b, deltas, skip):
    tl = min(FILTER_TILE, 2 * seq)
    n_tiles = 2 * seq // tl
    const = lambda i: (0, 0)
    full = lambda a: pl.BlockSpec(a.shape, const)
    w1t = w1.T
    args = (band, w1t[:, 0:1], w1t[:, 1:1 + FILTER_BANDS], w1t[:, 1 + FILTER_BANDS:], b1, fr,
            w2t, b2, w3f, w3b, deltas, skip)
    return pl.pallas_call(
        functools.partial(_hyena_filter_kernel, seq=seq, n_tiles=n_tiles),
        grid=(n_tiles,),
        in_specs=[full(a) for a in args],
        out_specs=[pl.BlockSpec((D_HY, tl), lambda i: (0, i)),
                   pl.BlockSpec((D_HY, 1), const),
                   pl.BlockSpec((D_HY, 1), const)],
        out_shape=[jax.ShapeDtypeStruct((D_HY, 2 * seq), F32),
                   jax.ShapeDtypeStruct((D_HY, 1), F32),
                   jax.ShapeDtypeStruct((D_HY, 1), F32)],
        scratch_shapes=[pltpu.VMEM((D_HY, 1), F32)],
        compiler_params=_params(1),
        name="hyena_filter",
    )(*args)


def _expand_filter(g_ref, tap, e_ref, nb):
    t = SEQ_BLOCK
    width = 2 * nb * t
    half = t // 2
    chunk = min(ROLL_CHUNK, width)
    for c in range(width // chunk):
        m0 = c * chunk
        lo = max(m0 - half, 0)
        win = g_ref[:, lo:m0 + chunk]
        if lo <= nb * t < m0 + chunk:
            col = lax.broadcasted_iota(jnp.int32, win.shape, 1)
            win = win + jnp.where(col == nb * t - lo, tap, 0.0)
        rolled = pltpu.roll(jnp.broadcast_to(win, (half, win.shape[1])), 0, axis=1,
                            stride=1, stride_axis=0)
        blk = rolled[:, m0 - lo:].astype(BF16)
        for w in range(chunk // half):
            piece = blk[:, w * half:(w + 1) * half]
            m_up = m0 + w * half
            r_up = (m_up // t) * t
            e_ref[r_up:r_up + half, m_up % t:m_up % t + half] = piece
            m_lo = m_up + half
            if m_lo < width:
                r_lo = (m_lo // t) * t + half
                e_ref[r_lo:r_lo + half, m_lo % t:m_lo % t + half] = piece


def _toeplitz_matmuls(a_ref, e_ref, o_ref, ak_ref, sh_ref, acc_ref, batch, nb):
    t = SEQ_BLOCK
    k = CONV_GROUP
    rows_a = batch * nb
    sh_ref[0:batch, :] = jnp.zeros((batch, t), F32)
    sh_ref[batch:batch + rows_a, :] = a_ref[...].astype(F32)
    sh_ref[batch + rows_a:, :] = jnp.zeros((batch, t), F32)
    a_shift = sh_ref[...].astype(BF16)
    ak_ref[...] = jnp.zeros_like(ak_ref)
    for e in range(k):
        if e % 2 == 0:
            ak_ref[e * batch:e * batch + rows_a, e * t:(e + 1) * t] = a_ref[...]
        else:
            r0 = (e - 1) * batch
            ak_ref[r0:r0 + rows_a + 2 * batch, e * t:(e + 1) * t] = a_shift
    acc_ref[...] = jnp.zeros_like(acc_ref)
    for q0 in range(0, 2 * nb, k):
        e_lo = 1 if q0 == 0 else 0
        d0 = q0 - nb
        j_lo = max(-d0, e_lo)
        j_hi = min(nb - d0, nb + k - 1)
        j_lo -= j_lo % 2
        j_hi += j_hi % 2
        r_out = batch + batch * (j_lo + d0)
        acc_ref[r_out:r_out + batch * (j_hi - j_lo), :] += jnp.dot(
            ak_ref[batch * j_lo:batch * j_hi, e_lo * t:k * t],
            e_ref[(q0 + e_lo) * t:(q0 + k) * t, :], preferred_element_type=F32)
    o_ref[...] = acc_ref[batch:batch + batch * nb, :].astype(o_ref.dtype)


def _hyena_conv_kernel(tap_ref, a_ref, g_ref, o_ref, e0_ref, e1_ref, ak_ref, sh_ref, acc_ref, *,
                       batch, nb):
    cg = a_ref.shape[0]
    first = pl.program_id(0) * cg
    _expand_filter(g_ref.at[0], tap_ref[first], e0_ref, nb)

    def body(kk, carry):
        c0 = 2 * kk
        _expand_filter(g_ref.at[c0 + 1], tap_ref[first + c0 + 1], e1_ref, nb)
        _toeplitz_matmuls(a_ref.at[c0], e0_ref, o_ref.at[c0], ak_ref, sh_ref, acc_ref, batch, nb)
        nxt = jnp.minimum(c0 + 2, cg - 1)
        _expand_filter(g_ref.at[nxt], tap_ref[first + nxt], e0_ref, nb)
        _toeplitz_matmuls(a_ref.at[c0 + 1], e1_ref, o_ref.at[c0 + 1], ak_ref, sh_ref, acc_ref,
                          batch, nb)
        return carry

    lax.fori_loop(0, cg // 2, body, 0)


def _hyena_conv(tap, st3, g3, batch, nb):
    t = SEQ_BLOCK
    rows = nb * batch
    cg = CONV_CHANNELS
    k = CONV_GROUP
    grp = lambda c: (c, 0, 0)
    expand = pltpu.VMEM((2 * nb * t, t), BF16)
    return pl.pallas_call(
        functools.partial(_hyena_conv_kernel, batch=batch, nb=nb),
        grid=(D_HY // cg,),
        in_specs=[pl.BlockSpec(memory_space=pltpu.SMEM),
                  pl.BlockSpec((cg, rows, t), grp),
                  pl.BlockSpec((cg, 1, 2 * nb * t), grp)],
        out_specs=pl.BlockSpec((cg, rows, t), grp),
        out_shape=jax.ShapeDtypeStruct((D_HY, rows, t), BF16),
        scratch_shapes=[expand, expand,
                        pltpu.VMEM((rows + k * batch, k * t), BF16),
                        pltpu.VMEM((rows + 2 * batch, t), F32),
                        pltpu.VMEM((rows + 2 * batch, t), F32)],
        compiler_params=_params(1),
        name="hyena_conv",
    )(tap, st3, g3)


def _dil_attn_kernel(bucket_ref, relb_ref, q_ref, k_ref, v_ref, kp_ref, vp_ref, kn_ref, vn_ref,
                     o_ref, lse_ref, kall_ref, vall_ref, bias_ref, *, n):
    n_res, tq, _ = q_ref.shape
    side = ATT_SIDE
    first = (pl.program_id(0) == 0) & (pl.program_id(1) == 0) & (pl.program_id(2) == 0)

    @pl.when(first)
    def _():
        bucket = bucket_ref[...]
        for h in range(N_ATT_HEADS):
            bias_ref[h] = jnp.full(bucket.shape, NEG_INF, F32)

        def body(bk, carry):
            hit = bucket == bk
            for h in range(N_ATT_HEADS):
                bias_ref[h] = jnp.where(hit, relb_ref[bk * N_ATT_HEADS + h], bias_ref[h])
            return carry

        lax.fori_loop(0, N_BUCKETS, body, 0)

    i = pl.program_id(2)
    lane = lax.broadcasted_iota(jnp.int32, (Q_BLOCK, LANES), 1)
    low_head = lane < HEAD_DIM
    head_sel = [low_head[0:1, :].astype(BF16), jnp.logical_not(low_head)[0:1, :].astype(BF16)]
    n_sub = tq // Q_BLOCK
    for rr in range(n_res):
        kall_ref[rr, 0:side, :] = kp_ref[rr]
        kall_ref[rr, side:side + tq, :] = k_ref[rr]
        kall_ref[rr, side + tq:, :] = kn_ref[rr]
        vall_ref[rr, 0:side, :] = vp_ref[rr]
        vall_ref[rr, side:side + tq, :] = v_ref[rr]
        vall_ref[rr, side + tq:, :] = vn_ref[rr]
        for j in range(n_sub):
            r0 = j * Q_BLOCK
            at_edge = j == 0 or j == n_sub - 1
            if at_edge:
                kpos = (i * tq + r0 - side
                        + lax.broadcasted_iota(jnp.int32, (1, 2 * Q_BLOCK), 1))
                in_seq = (kpos >= 0) & (kpos < n)
            lse_rows = jnp.zeros((Q_BLOCK, LANES), F32)
            for hp in range(N_ATT_HEADS // 2):
                cols = slice(hp * LANES, (hp + 1) * LANES)
                q2 = q_ref[rr, r0:r0 + Q_BLOCK, cols]
                k2 = kall_ref[rr, r0:r0 + 2 * Q_BLOCK, cols]
                v2 = vall_ref[rr, r0:r0 + 2 * Q_BLOCK, cols]
                v2e = jnp.concatenate([v2, jnp.ones_like(v2)], axis=1)
                outs = []
                for hh in range(2):
                    qh = q2 * head_sel[hh]
                    s = lax.dot_general(qh, k2, (((1,), (1,)), ((), ())),
                                        preferred_element_type=F32)
                    s = s + bias_ref[2 * hp + hh]
                    if at_edge:
                        s = jnp.where(in_seq, s, NEG_INF)
                    mx = jnp.max(s, axis=-1, keepdims=True)
                    p = jnp.exp(s - mx).astype(BF16)
                    od = jnp.dot(p, v2e, preferred_element_type=F32)
                    den = od[:, LANES:]
                    outs.append(od[:, 0:LANES] * (1.0 / den))
                    lse_rows = jnp.where(lane == 2 * hp + hh, mx + jnp.log(den), lse_rows)
                o_ref[rr, r0:r0 + Q_BLOCK, cols] = jnp.where(
                    low_head, outs[0], outs[1]).astype(BF16)
            lse_ref[rr, r0:r0 + Q_BLOCK, :] = lse_rows


def _attn_bucket_table(dil):
    half = N_BUCKETS // 2
    max_exact = half // 2
    qi = np.arange(Q_BLOCK)[:, None]
    kj = np.arange(2 * Q_BLOCK)[None, :]
    rel_steps = kj - ATT_SIDE - qi
    rel = rel_steps * dil
    ret = np.where(rel > 0, half, 0)
    na = np.abs(rel)
    large = max_exact + (np.log(np.maximum(na, 1).astype(np.float32) / np.float32(max_exact))
                         / np.float32(math.log(MAX_DISTANCE / max_exact))
                         * np.float32(half - max_exact)).astype(np.int32)
    large = np.minimum(large, half - 1)
    bucket = ret + np.where(na < max_exact, na, large)
    return np.where(np.abs(rel_steps) <= ATT_SIDE, bucket, -1).astype(np.int32)


def _dil_attn(qkv, relb_flat, dil):
    batch, _, n, _ = qkv.shape
    tq = min(n, ATT_ROWS)
    nt = n // tq
    n_res = min(dil, ATT_ROWS // tq)
    side = ATT_SIDE
    per = tq // side
    bucket = jnp.asarray(_attn_bucket_table(dil))
    main = lambda c: (lambda b, r, i: (b, r, i, c))
    prev = lambda c: (lambda b, r, i: (b, r, jnp.maximum(i * per - 1, 0), c))
    nxt = lambda c: (lambda b, r, i: (b, r, jnp.minimum((i + 1) * per, n // side - 1), c))
    blk = lambda rows, imap: pl.BlockSpec((None, n_res, rows, D_ATT), imap)
    return pl.pallas_call(
        functools.partial(_dil_attn_kernel, n=n),
        grid=(batch, dil // n_res, nt),
        in_specs=[pl.BlockSpec(bucket.shape, lambda b, r, i: (0, 0)),
                  pl.BlockSpec(memory_space=pltpu.SMEM),
                  blk(tq, main(0)), blk(tq, main(1)), blk(tq, main(2)),
                  blk(side, prev(1)), blk(side, prev(2)),
                  blk(side, nxt(1)), blk(side, nxt(2))],
        out_specs=[blk(tq, main(0)), pl.BlockSpec((None, n_res, tq, LANES), main(0))],
        out_shape=[jax.ShapeDtypeStruct((batch, dil, n, D_ATT), BF16),
                   jax.ShapeDtypeStruct((batch, dil, n, LANES), F32)],
        scratch_shapes=[pltpu.VMEM((n_res, tq + 2 * side, D_ATT), BF16),
                        pltpu.VMEM((n_res, tq + 2 * side, D_ATT), BF16),
                        pltpu.VMEM((N_ATT_HEADS, Q_BLOCK, 2 * Q_BLOCK), F32)],
        compiler_params=_params(3),
        name=f"dil_attn_d{dil}",
    )(bucket, relb_flat, qkv, qkv, qkv, qkv, qkv, qkv, qkv)


def _mem_kv_kernel(m_ref, g_ref, w_ref, bd_ref, kg_ref, kv_ref):
    x = m_ref[...]
    ms = jnp.mean(x * x, axis=-1, keepdims=True)
    h = (x * lax.rsqrt(ms + RMS_EPS) * g_ref[...]).astype(BF16)
    kv = jnp.dot(h, w_ref[...], preferred_element_type=F32)
    k = kv[:, 0:D_MEM]
    kn = k * lax.rsqrt(_head_mean_square(k, bd_ref[...]) + RMS_EPS) * kg_ref[...]
    kv_ref[:, 0:D_MEM] = kn.astype(BF16)
    kv_ref[:, D_MEM:] = kv[:, D_MEM:].astype(BF16)


def _mem_kv(mem, g, w, bd, kg):
    batch, n_mem, d = mem.shape
    const = lambda b: (0, 0)
    return pl.pallas_call(
        _mem_kv_kernel,
        grid=(batch,),
        in_specs=[pl.BlockSpec((None, n_mem, d), lambda b: (b, 0, 0)),
                  pl.BlockSpec((1, d), const),
                  pl.BlockSpec((d, 2 * D_MEM), const),
                  pl.BlockSpec((MXU_EDGE, MXU_EDGE), const),
                  pl.BlockSpec((1, D_MEM), const)],
        out_specs=pl.BlockSpec((None, n_mem, 2 * D_MEM), lambda b: (b, 0, 0)),
        out_shape=jax.ShapeDtypeStruct((batch, n_mem, 2 * D_MEM), BF16),
        compiler_params=_params(1),
        name="mem_kv",
    )(mem, g, w, bd, kg)


def _out_proj_kernel(x_ref, ct_ref, nrm_ref, x0_ref, o1_ref, l1_ref, o4_ref, l4_ref,
                     o16_ref, l16_ref, sel_ref, qm_ref, kv_ref, gs_ref, w_ref, y_ref,
                     so4_ref, sl4_ref, so16_ref, sl16_ref, tmp_ref, mixed_ref):
    nbat, t, d = x_ref.shape
    rows = nbat * t

    def gated(value, cols):
        gate = gs_ref[:, :, cols].astype(F32).reshape(rows, cols.stop - cols.start)
        mixed_ref[:, cols] = (value * gate).astype(BF16)

    conv = (ct_ref[...].astype(F32) * nrm_ref[...]).T
    y_hy = x0_ref[...].astype(F32).reshape(rows, D_HY) * conv
    gated(y_hy, slice(0, D_HY))

    tq = t // 4

    def to_natural(src4, src16, dst4, dst16, c, cols, slot):
        for bb in range(nbat):
            tmp = tmp_ref.at[slot * nbat + bb]
            for r4 in range(4):
                dst4[c, pl.ds(bb * t + r4, tq, stride=4), :] = src4[bb, r4, :, cols].astype(F32)
                for a in range(4):
                    tmp[r4, pl.ds(a, tq // 4, stride=4), :] = (
                        src16[bb, 4 * a + r4, :, cols].astype(F32))
            for r4 in range(4):
                dst16[c, pl.ds(bb * t + r4, tq, stride=4), :] = tmp[r4]

    n_chunks = D_ATT // LANES
    to_natural(l4_ref, l16_ref, sl4_ref, sl16_ref, 0, slice(0, LANES), n_chunks)
    l1 = l1_ref[...].reshape(rows, LANES)
    l2, l3 = sl4_ref[0], sl16_ref[0]
    top = jnp.maximum(jnp.maximum(l1, l2), l3)
    e1, e2, e3 = jnp.exp(l1 - top), jnp.exp(l2 - top), jnp.exp(l3 - top)
    inv = 1.0 / (e1 + e2 + e3)
    spread = [jnp.dot((e * inv).astype(BF16), sel_ref[...], preferred_element_type=F32)
              for e in (e1, e2, e3)]
    for c in range(n_chunks):
        cols = slice(c * LANES, (c + 1) * LANES)
        to_natural(o4_ref, o16_ref, so4_ref, so16_ref, c, cols, c)
        y_att = (spread[0][:, cols] * o1_ref[:, :, cols].astype(F32).reshape(rows, LANES)
                 + spread[1][:, cols] * so4_ref[c] + spread[2][:, cols] * so16_ref[c])
        gated(y_att, slice(D_HY + c * LANES, D_HY + (c + 1) * LANES))

    lane = lax.broadcasted_iota(jnp.int32, (t, LANES), 1)
    low_head = lane < HEAD_DIM
    head_sel = [low_head[0:1, :].astype(BF16), jnp.logical_not(low_head)[0:1, :].astype(BF16)]
    for hp in range(N_MEM_HEADS // 2):
        cols = slice(hp * LANES, (hp + 1) * LANES)
        mem_rows = []
        for bb in range(nbat):
            q2 = qm_ref[bb, :, cols]
            k2 = kv_ref[bb, :, cols]
            v2 = kv_ref[bb, :, D_MEM + hp * LANES:D_MEM + (hp + 1) * LANES]
            v2e = jnp.concatenate([v2, jnp.ones_like(v2)], axis=1)
            outs = []
            for hh in range(2):
                qh = q2 * head_sel[hh]
                s = lax.dot_general(qh, k2, (((1,), (1,)), ((), ())),
                                    preferred_element_type=F32)
                mx = jnp.max(s, axis=-1, keepdims=True)
                p = jnp.exp(s - mx).astype(BF16)
                od = jnp.dot(p, v2e, preferred_element_type=F32)
                outs.append(od[:, 0:LANES] * (1.0 / od[:, LANES:]))
            mem_rows.append(jnp.where(low_head, outs[0], outs[1]))
        base = D_HY + D_ATT + hp * LANES
        gated(jnp.concatenate(mem_rows, axis=0), slice(base, base + LANES))

    y = x_ref[...].reshape(rows, d) + jnp.dot(mixed_ref[...], w_ref[...],
                                              preferred_element_type=F32)
    y_ref[...] = y.reshape(nbat, t, d)


def _out_proj(x, conv_t, nrm, x0, att, qm, kv, gs, w):
    batch, seq, d = x.shape
    t = SEQ_BLOCK
    nb = seq // t
    nbat = TILE_BATCH
    n_chunks = D_ATT // LANES
    const = lambda b, j: (0, 0)
    tok = lambda width: pl.BlockSpec((nbat, t, width), lambda b, j: (b, j, 0))
    res = lambda dil, width: pl.BlockSpec((nbat, dil, t // dil, width),
                                          lambda b, j: (b, 0, j, 0))
    nat = lambda width: pl.BlockSpec((nbat, None, t, width), lambda b, j: (b, 0, j, 0))
    (o1, l1), (o4, l4), (o16, l16) = att
    natural = pltpu.VMEM((n_chunks, nbat * t, LANES), F32)
    natural_lse = pltpu.VMEM((1, nbat * t, LANES), F32)
    head_of_lane = np.arange(D_ATT) // HEAD_DIM
    sel = jnp.asarray(np.arange(LANES)[:, None] == head_of_lane[None, :], BF16)
    return pl.pallas_call(
        _out_proj_kernel,
        grid=(batch // nbat, nb),
        in_specs=[tok(d),
                  pl.BlockSpec((D_HY, nbat * t), lambda b, j: (0, j * (batch // nbat) + b)),
                  pl.BlockSpec((D_HY, 1), const),
                  tok(D_HY), nat(D_ATT), nat(LANES), res(4, D_ATT), res(4, LANES),
                  res(16, D_ATT), res(16, LANES),
                  pl.BlockSpec((LANES, D_ATT), const),
                  tok(D_MEM),
                  pl.BlockSpec((nbat, N_MEM, 2 * D_MEM), lambda b, j: (b, 0, 0)),
                  tok(D_MIX),
                  pl.BlockSpec((D_MIX, d), const)],
        out_specs=tok(d),
        out_shape=jax.ShapeDtypeStruct((batch, seq, d), F32),
        scratch_shapes=[natural, natural_lse, natural, natural_lse,
                        pltpu.VMEM(((n_chunks + 1) * nbat, 4, t // 4, LANES), F32),
                        pltpu.VMEM((nbat * t, D_MIX), BF16)],
        compiler_params=_params(2),
        name="out_proj",
    )(x, conv_t, nrm, x0, o1, l1, o4, l4, o16, l16, sel, qm, kv, gs, w)


def _encoder_layer(x, mem, norm_in, w_in, hy_conv_w, hy_conv_b, hy_filt_w1, hy_filt_b1,
                   hy_filt_freq, hy_filt_w2, hy_filt_b2, hy_filt_w3, hy_skip, att_q_norm,
                   att_k_norm, mem_norm, w_mem_kv, mem_q_norm, mem_k_norm, w_out, rel_bias):
    batch, seq, d = x.shape
    assert seq % (SEQ_BLOCK * 2) == 0 and batch % TILE_BATCH == 0 and batch % SUBLANES == 0
    assert all(seq // dil >= Q_BLOCK for _, dil in DILATED_CONFIGS)
    assert all(win // (2 * dil) == ATT_SIDE for win, dil in DILATED_CONFIGS)
    assert tuple(dil for _, dil in DILATED_CONFIGS) == (1, 4, 16)
    nb = seq // SEQ_BLOCK
    scale = HEAD_DIM ** -0.5

    blk = np.arange(MXU_EDGE) // HEAD_DIM
    bd = jnp.asarray((blk[:, None] == blk[None, :]).astype(np.float32) / HEAD_DIM, BF16)
    qkg = jnp.concatenate([jnp.tile(att_q_norm.astype(F32), N_ATT_HEADS) * scale,
                           jnp.tile(att_k_norm.astype(F32), N_ATT_HEADS)])[None, :]
    mqg = (jnp.tile(mem_q_norm.astype(F32), N_MEM_HEADS) * scale)[None, :]
    mkg = jnp.tile(mem_k_norm.astype(F32), N_MEM_HEADS)[None, :]

    w_normed = (norm_in.astype(F32)[:, None] * w_in.astype(F32)).astype(BF16)
    st, x0, q1, q4, q16, qm, gs = _in_proj(x, w_normed, bd, qkg, mqg, hy_conv_w.astype(F32),
                                           hy_conv_b.astype(F32)[None, :])

    band = jnp.linspace(1e-4, FILTER_BANDS - 1, FILTER_BANDS, dtype=F32)[:, None]
    deltas = jnp.abs(jnp.linspace(math.log(DECAY_TARGET) / SLOW_DECAY_PCT,
                                  math.log(DECAY_TARGET) / FAST_DECAY_PCT, D_HY, dtype=F32))
    w3t = hy_filt_w3.astype(F32).T
    g, nrm, tap = _hyena_filter(seq, band, hy_filt_w1.astype(F32), hy_filt_b1.astype(F32)[:, None],
                           hy_filt_freq.astype(F32)[:, None], hy_filt_w2.astype(F32).T,
                           hy_filt_b2.astype(F32)[:, None], w3t[0:D_HY], w3t[D_HY:],
                           deltas[:, None], hy_skip.astype(F32)[:, None])

    conv_t = _hyena_conv(tap.reshape(D_HY), st.reshape(D_HY, nb * batch, SEQ_BLOCK),
                         g.reshape(D_HY, 1, 2 * seq), batch, nb)
    conv_t = conv_t.reshape(D_HY, nb * batch * SEQ_BLOCK)

    relb_flat = rel_bias.astype(F32).reshape(-1)
    att = [_dil_attn(q1.reshape(batch, 1, seq, 3 * D_ATT), relb_flat, 1),
           _dil_attn(q4, relb_flat, 4),
           _dil_attn(q16, relb_flat, 16)]

    kv = _mem_kv(mem, mem_norm[None, :], w_mem_kv.astype(BF16), bd, mkg)

    return _out_proj(x, conv_t, nrm, x0, att, qm, kv, gs, w_out.astype(BF16))


def kernel(x_prompt, x_sample, mem_prompt, mem_sample, norm_in, w_in, hy_conv_w, hy_conv_b, hy_filt_w1, hy_filt_b1, hy_filt_freq, hy_filt_w2, hy_filt_b2, hy_filt_w3, hy_skip, att_q_norm, att_k_norm, mem_norm, w_mem_kv, mem_q_norm, mem_k_norm, w_out, rel_bias):
    def trunk(x, mem):
        for l in range(norm_in.shape[0]):
            x = _encoder_layer(x, mem, norm_in[l], w_in[l], hy_conv_w[l], hy_conv_b[l],
                               hy_filt_w1[l], hy_filt_b1[l], hy_filt_freq[l], hy_filt_w2[l],
                               hy_filt_b2[l], hy_filt_w3[l], hy_skip[l], att_q_norm[l],
                               att_k_norm[l], mem_norm[l], w_mem_kv[l], mem_q_norm[l],
                               mem_k_norm[l], w_out[l], rel_bias)
        return x

    return (trunk(x_prompt, mem_prompt), trunk(x_sample, mem_sample))
```

```python
import functools
import math

import numpy as np
import jax
import jax.numpy as jnp
from jax import lax
from jax.experimental import pallas as pl
from jax.experimental.pallas import tpu as pltpu

F32 = jnp.float32
BF16 = jnp.bfloat16

N_MEM = 256
HEAD_DIM = 64
D_HY = 384
N_ATT_HEADS = 6
D_ATT = N_ATT_HEADS * HEAD_DIM
N_MEM_HEADS = 4
D_MEM = N_MEM_HEADS * HEAD_DIM
D_MIX = D_HY + D_ATT + D_MEM
FILTER_BANDS = 16
FILTER_HIDDEN = 64
DECAY_TARGET = 1e-2
FAST_DECAY_PCT = 0.3
SLOW_DECAY_PCT = 1.5
DILATED_CONFIGS = ((128, 1), (512, 4), (2048, 16))
N_BUCKETS = 32
MAX_DISTANCE = 1024
RMS_EPS = 1e-6
NEG_INF = -1e30

SUBLANES = 8
LANES = 128
MXU_EDGE = 256

SEQ_BLOCK = MXU_EDGE
TILE_BATCH = 2
Q_BLOCK = 128
ATT_SIDE = 64
ATT_ROWS = 2048
FILTER_TILE = 2048
ROLL_CHUNK = 1024
CONV_CHANNELS = 32
CONV_GROUP = 2
VMEM_LIMIT = 56 * 1024 * 1024


def _params(n_axes):
    return pltpu.CompilerParams(dimension_semantics=("arbitrary",) * n_axes,
                                vmem_limit_bytes=VMEM_LIMIT)


def _head_mean_square(z, bd):
    return jnp.dot((z * z).astype(BF16), bd, preferred_element_type=F32)


def _in_proj_kernel(x_ref, xp_ref, xn_ref, w_ref, bd_ref, qkg_ref, mqg_ref, cw_ref, cb_ref,
                    st_ref, x0_ref, q1_ref, q4_ref, q16_ref, qm_ref, gs_ref,
                    h_ref, stage_ref, res4_ref, *, n_blocks):
    i0 = 3 * D_HY
    i1 = i0 + 3 * D_ATT
    i2 = i1 + D_MEM
    j = pl.program_id(1)
    nbat, t, d = x_ref.shape
    halo = xp_ref.shape[1]
    rows = nbat * t

    def normed(xv):
        ms = jnp.mean(xv * xv, axis=-1, keepdims=True)
        return (xv * lax.rsqrt(ms + RMS_EPS)).astype(BF16)

    h_ref[0:rows, :] = normed(x_ref[...].reshape(rows, d))
    h_ref[rows:rows + nbat * halo, :] = normed(xp_ref[...].reshape(nbat * halo, d))
    h_ref[rows + nbat * halo:, :] = normed(xn_ref[...].reshape(nbat * halo, d))
    hm = h_ref[0:rows, :]
    bd = bd_ref[...]

    zall = jnp.dot(h_ref[...], w_ref[:, 0:i0], preferred_element_type=F32)
    za = jnp.dot(hm, w_ref[:, i0:i1], preferred_element_type=F32)

    not_first = (j > 0).astype(F32)
    not_last = (j < n_blocks - 1).astype(F32)
    row = lax.broadcasted_iota(jnp.int32, (t, i0), 0)
    for bb in range(nbat):
        z = zall[bb * t:(bb + 1) * t]
        r_before = rows + bb * halo + halo - 1
        r_after = rows + nbat * halo + bb * halo
        z_before = zall[r_before:r_before + 1] * not_first
        z_after = zall[r_after:r_after + 1] * not_last
        zm1 = jnp.where(row == 0, z_before, pltpu.roll(z, 1, axis=0))
        zp1 = jnp.where(row == t - 1, z_after, pltpu.roll(z, t - 1, axis=0))
        u = zm1 * cw_ref[0:1, :] + z * cw_ref[1:2, :] + zp1 * cw_ref[2:3, :] + cb_ref[...]
        x0 = u[:, 0:D_HY]
        s = u[:, D_HY:2 * D_HY] * u[:, 2 * D_HY:]
        x0_ref[bb] = x0.astype(BF16)
        st_ref[:, bb * t:(bb + 1) * t] = s.T.astype(BF16)

    gate = jnp.dot(hm, w_ref[:, i2:], preferred_element_type=F32)

    per_edge = MXU_EDGE // LANES
    for c in range(2 * D_ATT // MXU_EDGE):
        sl = slice(c * MXU_EDGE, (c + 1) * MXU_EDGE)
        z = za[:, sl]
        zn = z * lax.rsqrt(_head_mean_square(z, bd) + RMS_EPS) * qkg_ref[:, sl]
        for k in range(per_edge):
            stage_ref[c * per_edge + k] = zn[:, k * LANES:(k + 1) * LANES]
    for c in range(2 * D_ATT // LANES, 3 * D_ATT // LANES):
        stage_ref[c] = za[:, c * LANES:(c + 1) * LANES]

    zq = jnp.dot(hm, w_ref[:, i1:i2], preferred_element_type=F32)

    half_gate = 0.5 * gate
    silu = half_gate * jnp.tanh(half_gate) + half_gate
    gs_ref[...] = silu.astype(BF16).reshape(nbat, t, D_MIX)

    tq = t // 4
    for c in range(3 * D_ATT // LANES):
        cols = slice(c * LANES, (c + 1) * LANES)
        for bb in range(nbat):
            q1_ref[bb, :, cols] = stage_ref[c, bb * t:(bb + 1) * t, :].astype(BF16)
            tmp = res4_ref.at[c * nbat + bb]
            for r4 in range(4):
                sub = stage_ref[c, pl.ds(bb * t + r4, tq, stride=4), :]
                q4_ref[bb, r4, :, cols] = sub.astype(BF16)
                tmp[r4] = sub
            for r4 in range(4):
                for a in range(4):
                    sub = tmp[r4, pl.ds(a, tq // 4, stride=4), :]
                    q16_ref[bb, 4 * a + r4, :, cols] = sub.astype(BF16)

    zq = zq * lax.rsqrt(_head_mean_square(zq, bd) + RMS_EPS) * mqg_ref[...]
    qm_ref[...] = zq.astype(BF16).reshape(nbat, t, D_MEM)


def _in_proj(x, w, bd, qkg, mqg, cw, cb):
    batch, seq, d = x.shape
    d_in = w.shape[1]
    t = SEQ_BLOCK
    nb = seq // t
    nbat = TILE_BATCH
    halo = SUBLANES
    per = t // halo
    n_chunks = 3 * D_ATT // LANES
    const = lambda b, j: (0, 0)
    tok = lambda width: pl.BlockSpec((nbat, t, width), lambda b, j: (b, j, 0))
    res = lambda dil: pl.BlockSpec((nbat, dil, t // dil, 3 * D_ATT), lambda b, j: (b, 0, j, 0))
    return pl.pallas_call(
        functools.partial(_in_proj_kernel, n_blocks=nb),
        grid=(batch // nbat, nb),
        in_specs=[tok(d),
                  pl.BlockSpec((nbat, halo, d), lambda b, j: (b, jnp.maximum(j * per - 1, 0), 0)),
                  pl.BlockSpec((nbat, halo, d),
                               lambda b, j: (b, jnp.minimum((j + 1) * per, seq // halo - 1), 0)),
                  pl.BlockSpec((d, d_in), const),
                  pl.BlockSpec((MXU_EDGE, MXU_EDGE), const),
                  pl.BlockSpec((1, 2 * D_ATT), const),
                  pl.BlockSpec((1, D_MEM), const),
                  pl.BlockSpec((3, 3 * D_HY), const),
                  pl.BlockSpec((1, 3 * D_HY), const)],
        out_specs=[pl.BlockSpec((D_HY, nbat * t), lambda b, j: (0, j * (batch // nbat) + b)),
                   tok(D_HY), tok(3 * D_ATT), res(4), res(16), tok(D_MEM), tok(D_MIX)],
        out_shape=[jax.ShapeDtypeStruct((D_HY, nb * batch * t), BF16),
                   jax.ShapeDtypeStruct((batch, seq, D_HY), BF16),
                   jax.ShapeDtypeStruct((batch, seq, 3 * D_ATT), BF16),
                   jax.ShapeDtypeStruct((batch, 4, seq // 4, 3 * D_ATT), BF16),
                   jax.ShapeDtypeStruct((batch, 16, seq // 16, 3 * D_ATT), BF16),
                   jax.ShapeDtypeStruct((batch, seq, D_MEM), BF16),
                   jax.ShapeDtypeStruct((batch, seq, D_MIX), BF16)],
        scratch_shapes=[pltpu.VMEM((nbat * (t + 2 * halo), d), BF16),
                        pltpu.VMEM((n_chunks, nbat * t, LANES), F32),
                        pltpu.VMEM((n_chunks * nbat, 4, t // 4, LANES), F32)],
        compiler_params=_params(2),
        name="in_proj",
    )(x, x, x, w, bd, qkg, mqg, cw, cb)


def _hyena_filter_kernel(band_ref, w1t_ref, w1c_ref, w1s_ref, b1_ref, fr_ref, w2t_ref, b2_ref,
                         w3f_ref, w3b_ref, dl_ref, skip_ref, g_ref, nrm_ref, tap_ref, ss_ref, *,
                         seq, n_tiles):
    i = pl.program_id(0)
    tl = g_ref.shape[1]
    hi = lax.Precision.HIGHEST
    m = i * tl + lax.broadcasted_iota(jnp.int32, (1, tl), 1)
    lag = m - seq
    pos = jnp.abs(lag).astype(F32)
    tt = pos / float(max(seq - 1, 1))
    ang = ((2.0 * math.pi / seq) * pos) * band_ref[...]
    pre = (w1t_ref[...] * tt
           + jnp.dot(w1c_ref[...], jnp.cos(ang), precision=hi, preferred_element_type=F32)
           + jnp.dot(w1s_ref[...], -jnp.sin(ang), precision=hi, preferred_element_type=F32))
    fr = fr_ref[...]
    h = jnp.sin(fr * (pre + b1_ref[...]))
    h = jnp.sin(fr * (jnp.dot(w2t_ref[...], h, precision=hi, preferred_element_type=F32)
                      + b2_ref[...]))
    fwd = jnp.dot(w3f_ref[...], h, precision=hi, preferred_element_type=F32)
    bwd = jnp.dot(w3b_ref[...], h, precision=hi, preferred_element_type=F32)
    decay = jnp.exp(-tt * dl_ref[...])
    val = jnp.where(lag > 0, fwd, jnp.where(lag < 0, bwd, fwd + bwd)) * decay
    val = jnp.where(m == 0, 0.0, val)
    g_ref[...] = val

    @pl.when(i == 0)
    def _():
        ss_ref[...] = jnp.zeros_like(ss_ref)

    ss_ref[...] += jnp.sum(val * val, axis=1, keepdims=True)

    @pl.when(i == n_tiles - 1)
    def _():
        nrm_ref[...] = lax.rsqrt(ss_ref[...] + 1e-12)
        tap_ref[...] = skip_ref[...] * jnp.sqrt(ss_ref[...] + 1e-12)


def _hyena_filter(seq, band, w1, b1, fr, w2t, b2, w3f, w3b, deltas, skip):
    tl = min(FILTER_TILE, 2 * seq)
    n_tiles = 2 * seq // tl
    const = lambda i: (0, 0)
    full = lambda a: pl.BlockSpec(a.shape, const)
    w1t = w1.T
    args = (band, w1t[:, 0:1], w1t[:, 1:1 + FILTER_BANDS], w1t[:, 1 + FILTER_BANDS:], b1, fr,
            w2t, b2, w3f, w3b, deltas, skip)
    return pl.pallas_call(
        functools.partial(_hyena_filter_kernel, seq=seq, n_tiles=n_tiles),
        grid=(n_tiles,),
        in_specs=[full(a) for a in args],
        out_specs=[pl.BlockSpec((D_HY, tl), lambda i: (0, i)),
                   pl.BlockSpec((D_HY, 1), const),
                   pl.BlockSpec((D_HY, 1), const)],
        out_shape=[jax.ShapeDtypeStruct((D_HY, 2 * seq), F32),
                   jax.ShapeDtypeStruct((D_HY, 1), F32),
                   jax.ShapeDtypeStruct((D_HY, 1), F32)],
        scratch_shapes=[pltpu.VMEM((D_HY, 1), F32)],
        compiler_params=_params(1),
        name="hyena_filter",
    )(*args)


def _expand_filter(g_ref, tap, e_ref, nb):
    t = SEQ_BLOCK
    width = 2 * nb * t
    half = t // 2
    chunk = min(ROLL_CHUNK, width)
    for c in range(width // chunk):
        m0 = c * chunk
        lo = max(m0 - half, 0)
        win = g_ref[:, lo:m0 + chunk]
        if lo <= nb * t < m0 + chunk:
            col = lax.broadcasted_iota(jnp.int32, win.shape, 1)
            win = win + jnp.where(col == nb * t - lo, tap, 0.0)
        rolled = pltpu.roll(jnp.broadcast_to(win, (half, win.shape[1])), 0, axis=1,
                            stride=1, stride_axis=0)
        blk = rolled[:, m0 - lo:].astype(BF16)
        for w in range(chunk // half):
            piece = blk[:, w * half:(w + 1) * half]
            m_up = m0 + w * half
            r_up = (m_up // t) * t
            e_ref[r_up:r_up + half, m_up % t:m_up % t + half] = piece
            m_lo = m_up + half
            if m_lo < width:
                r_lo = (m_lo // t) * t + half
                e_ref[r_lo:r_lo + half, m_lo % t:m_lo % t + half] = piece


def _toeplitz_matmuls(a_ref, e_ref, o_ref, ak_ref, sh_ref, acc_ref, batch, nb):
    t = SEQ_BLOCK
    k = CONV_GROUP
    rows_a = batch * nb
    sh_ref[0:batch, :] = jnp.zeros((batch, t), F32)
    sh_ref[batch:batch + rows_a, :] = a_ref[...].astype(F32)
    sh_ref[batch + rows_a:, :] = jnp.zeros((batch, t), F32)
    a_shift = sh_ref[...].astype(BF16)
    ak_ref[...] = jnp.zeros_like(ak_ref)
    for e in range(k):
        if e % 2 == 0:
            ak_ref[e * batch:e * batch + rows_a, e * t:(e + 1) * t] = a_ref[...]
        else:
            r0 = (e - 1) * batch
            ak_ref[r0:r0 + rows_a + 2 * batch, e * t:(e + 1) * t] = a_shift
    acc_ref[...] = jnp.zeros_like(acc_ref)
    for q0 in range(0, 2 * nb, k):
        e_lo = 1 if q0 == 0 else 0
        d0 = q0 - nb
        j_lo = max(-d0, e_lo)
        j_hi = min(nb - d0, nb + k - 1)
        j_lo -= j_lo % 2
        j_hi += j_hi % 2
        r_out = batch + batch * (j_lo + d0)
        acc_ref[r_out:r_out + batch * (j_hi - j_lo), :] += jnp.dot(
            ak_ref[batch * j_lo:batch * j_hi, e_lo * t:k * t],
            e_ref[(q0 + e_lo) * t:(q0 + k) * t, :], preferred_element_type=F32)
    o_ref[...] = acc_ref[batch:batch + batch * nb, :].astype(o_ref.dtype)


def _hyena_conv_kernel(tap_ref, a_ref, g_ref, o_ref, e0_ref, e1_ref, ak_ref, sh_ref, acc_ref, *,
                       batch, nb):
    cg = a_ref.shape[0]
    first = pl.program_id(0) * cg
    _expand_filter(g_ref.at[0], tap_ref[first], e0_ref, nb)

    def body(kk, carry):
        c0 = 2 * kk
        _expand_filter(g_ref.at[c0 + 1], tap_ref[first + c0 + 1], e1_ref, nb)
        _toeplitz_matmuls(a_ref.at[c0], e0_ref, o_ref.at[c0], ak_ref, sh_ref, acc_ref, batch, nb)
        nxt = jnp.minimum(c0 + 2, cg - 1)
        _expand_filter(g_ref.at[nxt], tap_ref[first + nxt], e0_ref, nb)
        _toeplitz_matmuls(a_ref.at[c0 + 1], e1_ref, o_ref.at[c0 + 1], ak_ref, sh_ref, acc_ref,
                          batch, nb)
        return carry

    lax.fori_loop(0, cg // 2, body, 0)


def _hyena_conv(tap, st3, g3, batch, nb):
    t = SEQ_BLOCK
    rows = nb * batch
    cg = CONV_CHANNELS
    k = CONV_GROUP
    grp = lambda c: (c, 0, 0)
    expand = pltpu.VMEM((2 * nb * t, t), BF16)
    return pl.pallas_call(
        functools.partial(_hyena_conv_kernel, batch=batch, nb=nb),
        grid=(D_HY // cg,),
        in_specs=[pl.BlockSpec(memory_space=pltpu.SMEM),
                  pl.BlockSpec((cg, rows, t), grp),
                  pl.BlockSpec((cg, 1, 2 * nb * t), grp)],
        out_specs=pl.BlockSpec((cg, rows, t), grp),
        out_shape=jax.ShapeDtypeStruct((D_HY, rows, t), BF16),
        scratch_shapes=[expand, expand,
                        pltpu.VMEM((rows + k * batch, k * t), BF16),
                        pltpu.VMEM((rows + 2 * batch, t), F32),
                        pltpu.VMEM((rows + 2 * batch, t), F32)],
        compiler_params=_params(1),
        name="hyena_conv",
    )(tap, st3, g3)


def _dil_attn_kernel(bucket_ref, relb_ref, q_ref, k_ref, v_ref, kp_ref, vp_ref, kn_ref, vn_ref,
                     o_ref, lse_ref, kall_ref, vall_ref, bias_ref, *, n):
    n_res, tq, _ = q_ref.shape
    side = ATT_SIDE
    first = (pl.program_id(0) == 0) & (pl.program_id(1) == 0) & (pl.program_id(2) == 0)

    @pl.when(first)
    def _():
        bucket = bucket_ref[...]
        for h in range(N_ATT_HEADS):
            bias_ref[h] = jnp.full(bucket.shape, NEG_INF, F32)

        def body(bk, carry):
            hit = bucket == bk
            for h in range(N_ATT_HEADS):
                bias_ref[h] = jnp.where(hit, relb_ref[bk * N_ATT_HEADS + h], bias_ref[h])
            return carry

        lax.fori_loop(0, N_BUCKETS, body, 0)

    i = pl.program_id(2)
    lane = lax.broadcasted_iota(jnp.int32, (Q_BLOCK, LANES), 1)
    low_head = lane < HEAD_DIM
    head_sel = [low_head[0:1, :].astype(BF16), jnp.logical_not(low_head)[0:1, :].astype(BF16)]
    n_sub = tq // Q_BLOCK
    for rr in range(n_res):
        kall_ref[rr, 0:side, :] = kp_ref[rr]
        kall_ref[rr, side:side + tq, :] = k_ref[rr]
        kall_ref[rr, side + tq:, :] = kn_ref[rr]
        vall_ref[rr, 0:side, :] = vp_ref[rr]
        vall_ref[rr, side:side + tq, :] = v_ref[rr]
        vall_ref[rr, side + tq:, :] = vn_ref[rr]
        for j in range(n_sub):
            r0 = j * Q_BLOCK
            at_edge = j == 0 or j == n_sub - 1
            if at_edge:
                kpos = (i * tq + r0 - side
                        + lax.broadcasted_iota(jnp.int32, (1, 2 * Q_BLOCK), 1))
                in_seq = (kpos >= 0) & (kpos < n)
            lse_rows = jnp.zeros((Q_BLOCK, LANES), F32)
            for hp in range(N_ATT_HEADS // 2):
                cols = slice(hp * LANES, (hp + 1) * LANES)
                q2 = q_ref[rr, r0:r0 + Q_BLOCK, cols]
                k2 = kall_ref[rr, r0:r0 + 2 * Q_BLOCK, cols]
                v2 = vall_ref[rr, r0:r0 + 2 * Q_BLOCK, cols]
                v2e = jnp.concatenate([v2, jnp.ones_like(v2)], axis=1)
                outs = []
                for hh in range(2):
                    qh = q2 * head_sel[hh]
                    s = lax.dot_general(qh, k2, (((1,), (1,)), ((), ())),
                                        preferred_element_type=F32)
                    s = s + bias_ref[2 * hp + hh]
                    if at_edge:
                        s = jnp.where(in_seq, s, NEG_INF)
                    mx = jnp.max(s, axis=-1, keepdims=True)
                    p = jnp.exp(s - mx).astype(BF16)
                    od = jnp.dot(p, v2e, preferred_element_type=F32)
                    den = od[:, LANES:]
                    outs.append(od[:, 0:LANES] * (1.0 / den))
                    lse_rows = jnp.where(lane == 2 * hp + hh, mx + jnp.log(den), lse_rows)
                o_ref[rr, r0:r0 + Q_BLOCK, cols] = jnp.where(
                    low_head, outs[0], outs[1]).astype(BF16)
            lse_ref[rr, r0:r0 + Q_BLOCK, :] = lse_rows


def _attn_bucket_table(dil):
    half = N_BUCKETS // 2
    max_exact = half // 2
    qi = np.arange(Q_BLOCK)[:, None]
    kj = np.arange(2 * Q_BLOCK)[None, :]
    rel_steps = kj - ATT_SIDE - qi
    rel = rel_steps * dil
    ret = np.where(rel > 0, half, 0)
    na = np.abs(rel)
    large = max_exact + (np.log(np.maximum(na, 1).astype(np.float32) / np.float32(max_exact))
                         / np.float32(math.log(MAX_DISTANCE / max_exact))
                         * np.float32(half - max_exact)).astype(np.int32)
    large = np.minimum(large, half - 1)
    bucket = ret + np.where(na < max_exact, na, large)
    return np.where(np.abs(rel_steps) <= ATT_SIDE, bucket, -1).astype(np.int32)


def _dil_attn(qkv, relb_flat, dil):
    batch, _, n, _ = qkv.shape
    tq = min(n, ATT_ROWS)
    nt = n // tq
    n_res = min(dil, ATT_ROWS // tq)
    side = ATT_SIDE
    per = tq // side
    bucket = jnp.asarray(_attn_bucket_table(dil))
    main = lambda c: (lambda b, r, i: (b, r, i, c))
    prev = lambda c: (lambda b, r, i: (b, r, jnp.maximum(i * per - 1, 0), c))
    nxt = lambda c: (lambda b, r, i: (b, r, jnp.minimum((i + 1) * per, n // side - 1), c))
    blk = lambda rows, imap: pl.BlockSpec((None, n_res, rows, D_ATT), imap)
    return pl.pallas_call(
        functools.partial(_dil_attn_kernel, n=n),
        grid=(batch, dil // n_res, nt),
        in_specs=[pl.BlockSpec(bucket.shape, lambda b, r, i: (0, 0)),
                  pl.BlockSpec(memory_space=pltpu.SMEM),
                  blk(tq, main(0)), blk(tq, main(1)), blk(tq, main(2)),
                  blk(side, prev(1)), blk(side, prev(2)),
                  blk(side, nxt(1)), blk(side, nxt(2))],
        out_specs=[blk(tq, main(0)), pl.BlockSpec((None, n_res, tq, LANES), main(0))],
        out_shape=[jax.ShapeDtypeStruct((batch, dil, n, D_ATT), BF16),
                   jax.ShapeDtypeStruct((batch, dil, n, LANES), F32)],
        scratch_shapes=[pltpu.VMEM((n_res, tq + 2 * side, D_ATT), BF16),
                        pltpu.VMEM((n_res, tq + 2 * side, D_ATT), BF16),
                        pltpu.VMEM((N_ATT_HEADS, Q_BLOCK, 2 * Q_BLOCK), F32)],
        compiler_params=_params(3),
        name=f"dil_attn_d{dil}",
    )(bucket, relb_flat, qkv, qkv, qkv, qkv, qkv, qkv, qkv)


def _mem_kv_kernel(m_ref, g_ref, w_ref, bd_ref, kg_ref, kv_ref):
    x = m_ref[...]
    ms = jnp.mean(x * x, axis=-1, keepdims=True)
    h = (x * lax.rsqrt(ms + RMS_EPS) * g_ref[...]).astype(BF16)
    kv = jnp.dot(h, w_ref[...], preferred_element_type=F32)
    k = kv[:, 0:D_MEM]
    kn = k * lax.rsqrt(_head_mean_square(k, bd_ref[...]) + RMS_EPS) * kg_ref[...]
    kv_ref[:, 0:D_MEM] = kn.astype(BF16)
    kv_ref[:, D_MEM:] = kv[:, D_MEM:].astype(BF16)


def _mem_kv(mem, g, w, bd, kg):
    batch, n_mem, d = mem.shape
    const = lambda b: (0, 0)
    return pl.pallas_call(
        _mem_kv_kernel,
        grid=(batch,),
        in_specs=[pl.BlockSpec((None, n_mem, d), lambda b: (b, 0, 0)),
                  pl.BlockSpec((1, d), const),
                  pl.BlockSpec((d, 2 * D_MEM), const),
                  pl.BlockSpec((MXU_EDGE, MXU_EDGE), const),
                  pl.BlockSpec((1, D_MEM), const)],
        out_specs=pl.BlockSpec((None, n_mem, 2 * D_MEM), lambda b: (b, 0, 0)),
        out_shape=jax.ShapeDtypeStruct((batch, n_mem, 2 * D_MEM), BF16),
        compiler_params=_params(1),
        name="mem_kv",
    )(mem, g, w, bd, kg)


def _out_proj_kernel(x_ref, ct_ref, nrm_ref, x0_ref, o1_ref, l1_ref, o4_ref, l4_ref,
                     o16_ref, l16_ref, sel_ref, qm_ref, kv_ref, gs_ref, w_ref, y_ref,
                     so4_ref, sl4_ref, so16_ref, sl16_ref, tmp_ref, mixed_ref):
    nbat, t, d = x_ref.shape
    rows = nbat * t

    def gated(value, cols):
        gate = gs_ref[:, :, cols].astype(F32).reshape(rows, cols.stop - cols.start)
        mixed_ref[:, cols] = (value * gate).astype(BF16)

    conv = (ct_ref[...].astype(F32) * nrm_ref[...]).T
    y_hy = x0_ref[...].astype(F32).reshape(rows, D_HY) * conv
    gated(y_hy, slice(0, D_HY))

    tq = t // 4

    def to_natural(src4, src16, dst4, dst16, c, cols, slot):
        for bb in range(nbat):
            tmp = tmp_ref.at[slot * nbat + bb]
            for r4 in range(4):
                dst4[c, pl.ds(bb * t + r4, tq, stride=4), :] = src4[bb, r4, :, cols].astype(F32)
                for a in range(4):
                    tmp[r4, pl.ds(a, tq // 4, stride=4), :] = (
                        src16[bb, 4 * a + r4, :, cols].astype(F32))
            for r4 in range(4):
                dst16[c, pl.ds(bb * t + r4, tq, stride=4), :] = tmp[r4]

    n_chunks = D_ATT // LANES
    to_natural(l4_ref, l16_ref, sl4_ref, sl16_ref, 0, slice(0, LANES), n_chunks)
    l1 = l1_ref[...].reshape(rows, LANES)
    l2, l3 = sl4_ref[0], sl16_ref[0]
    top = jnp.maximum(jnp.maximum(l1, l2), l3)
    e1, e2, e3 = jnp.exp(l1 - top), jnp.exp(l2 - top), jnp.exp(l3 - top)
    inv = 1.0 / (e1 + e2 + e3)
    spread = [jnp.dot((e * inv).astype(BF16), sel_ref[...], preferred_element_type=F32)
              for e in (e1, e2, e3)]
    for c in range(n_chunks):
        cols = slice(c * LANES, (c + 1) * LANES)
        to_natural(o4_ref, o16_ref, so4_ref, so16_ref, c, cols, c)
        y_att = (spread[0][:, cols] * o1_ref[:, :, cols].astype(F32).reshape(rows, LANES)
                 + spread[1][:, cols] * so4_ref[c] + spread[2][:, cols] * so16_ref[c])
        gated(y_att, slice(D_HY + c * LANES, D_HY + (c + 1) * LANES))

    lane = lax.broadcasted_iota(jnp.int32, (t, LANES), 1)
    low_head = lane < HEAD_DIM
    head_sel = [low_head[0:1, :].astype(BF16), jnp.logical_not(low_head)[0:1, :].astype(BF16)]
    for hp in range(N_MEM_HEADS // 2):
        cols = slice(hp * LANES, (hp + 1) * LANES)
        mem_rows = []
        for bb in range(nbat):
            q2 = qm_ref[bb, :, cols]
            k2 = kv_ref[bb, :, cols]
            v2 = kv_ref[bb, :, D_MEM + hp * LANES:D_MEM + (hp + 1) * LANES]
            v2e = jnp.concatenate([v2, jnp.ones_like(v2)], axis=1)
            outs = []
            for hh in range(2):
                qh = q2 * head_sel[hh]
                s = lax.dot_general(qh, k2, (((1,), (1,)), ((), ())),
                                    preferred_element_type=F32)
                mx = jnp.max(s, axis=-1, keepdims=True)
                p = jnp.exp(s - mx).astype(BF16)
                od = jnp.dot(p, v2e, preferred_element_type=F32)
                outs.append(od[:, 0:LANES] * (1.0 / od[:, LANES:]))
            mem_rows.append(jnp.where(low_head, outs[0], outs[1]))
        base = D_HY + D_ATT + hp * LANES
        gated(jnp.concatenate(mem_rows, axis=0), slice(base, base + LANES))

    y = x_ref[...].reshape(rows, d) + jnp.dot(mixed_ref[...], w_ref[...],
                                              preferred_element_type=F32)
    y_ref[...] = y.reshape(nbat, t, d)


def _out_proj(x, conv_t, nrm, x0, att, qm, kv, gs, w):
    batch, seq, d = x.shape
    t = SEQ_BLOCK
    nb = seq // t
    nbat = TILE_BATCH
    n_chunks = D_ATT // LANES
    const = lambda b, j: (0, 0)
    tok = lambda width: pl.BlockSpec((nbat, t, width), lambda b, j: (b, j, 0))
    res = lambda dil, width: pl.BlockSpec((nbat, dil, t // dil, width),
                                          lambda b, j: (b, 0, j, 0))
    nat = lambda width: pl.BlockSpec((nbat, None, t, width), lambda b, j: (b, 0, j, 0))
    (o1, l1), (o4, l4), (o16, l16) = att
    natural = pltpu.VMEM((n_chunks, nbat * t, LANES), F32)
    natural_lse = pltpu.VMEM((1, nbat * t, LANES), F32)
    head_of_lane = np.arange(D_ATT) // HEAD_DIM
    sel = jnp.asarray(np.arange(LANES)[:, None] == head_of_lane[None, :], BF16)
    return pl.pallas_call(
        _out_proj_kernel,
        grid=(batch // nbat, nb),
        in_specs=[tok(d),
                  pl.BlockSpec((D_HY, nbat * t), lambda b, j: (0, j * (batch // nbat) + b)),
                  pl.BlockSpec((D_HY, 1), const),
                  tok(D_HY), nat(D_ATT), nat(LANES), res(4, D_ATT), res(4, LANES),
                  res(16, D_ATT), res(16, LANES),
                  pl.BlockSpec((LANES, D_ATT), const),
                  tok(D_MEM),
                  pl.BlockSpec((nbat, N_MEM, 2 * D_MEM), lambda b, j: (b, 0, 0)),
                  tok(D_MIX),
                  pl.BlockSpec((D_MIX, d), const)],
        out_specs=tok(d),
        out_shape=jax.ShapeDtypeStruct((batch, seq, d), F32),
        scratch_shapes=[natural, natural_lse, natural, natural_lse,
                        pltpu.VMEM(((n_chunks + 1) * nbat, 4, t // 4, LANES), F32),
                        pltpu.VMEM((nbat * t, D_MIX), BF16)],
        compiler_params=_params(2),
        name="out_proj",
    )(x, conv_t, nrm, x0, o1, l1, o4, l4, o16, l16, sel, qm, kv, gs, w)


def _encoder_layer(x, mem, norm_in, w_in, hy_conv_w, hy_conv_b, hy_filt_w1, hy_filt_b1,
                   hy_filt_freq, hy_filt_w2, hy_filt_b2, hy_filt_w3, hy_skip, att_q_norm,
                   att_k_norm, mem_norm, w_mem_kv, mem_q_norm, mem_k_norm, w_out, rel_bias):
    batch, seq, d = x.shape
    assert seq % (SEQ_BLOCK * 2) == 0 and batch % TILE_BATCH == 0 and batch % SUBLANES == 0
    assert all(seq // dil >= Q_BLOCK for _, dil in DILATED_CONFIGS)
    assert all(win // (2 * dil) == ATT_SIDE for win, dil in DILATED_CONFIGS)
    assert tuple(dil for _, dil in DILATED_CONFIGS) == (1, 4, 16)
    nb = seq // SEQ_BLOCK
    scale = HEAD_DIM ** -0.5

    blk = np.arange(MXU_EDGE) // HEAD_DIM
    bd = jnp.asarray((blk[:, None] == blk[None, :]).astype(np.float32) / HEAD_DIM, BF16)
    qkg = jnp.concatenate([jnp.tile(att_q_norm.astype(F32), N_ATT_HEADS) * scale,
                           jnp.tile(att_k_norm.astype(F32), N_ATT_HEADS)])[None, :]
    mqg = (jnp.tile(mem_q_norm.astype(F32), N_MEM_HEADS) * scale)[None, :]
    mkg = jnp.tile(mem_k_norm.astype(F32), N_MEM_HEADS)[None, :]

    w_normed = (norm_in.astype(F32)[:, None] * w_in.astype(F32)).astype(BF16)
    st, x0, q1, q4, q16, qm, gs = _in_proj(x, w_normed, bd, qkg, mqg, hy_conv_w.astype(F32),
                                           hy_conv_b.astype(F32)[None, :])

    band = jnp.linspace(1e-4, FILTER_BANDS - 1, FILTER_BANDS, dtype=F32)[:, None]
    deltas = jnp.abs(jnp.linspace(math.log(DECAY_TARGET) / SLOW_DECAY_PCT,
                                  math.log(DECAY_TARGET) / FAST_DECAY_PCT, D_HY, dtype=F32))
    w3t = hy_filt_w3.astype(F32).T
    g, nrm, tap = _hyena_filter(seq, band, hy_filt_w1.astype(F32), hy_filt_b1.astype(F32)[:, None],
                           hy_filt_freq.astype(F32)[:, None], hy_filt_w2.astype(F32).T,
                           hy_filt_b2.astype(F32)[:, None], w3t[0:D_HY], w3t[D_HY:],
                           deltas[:, None], hy_skip.astype(F32)[:, None])

    conv_t = _hyena_conv(tap.reshape(D_HY), st.reshape(D_HY, nb * batch, SEQ_BLOCK),
                         g.reshape(D_HY, 1, 2 * seq), batch, nb)
    conv_t = conv_t.reshape(D_HY, nb * batch * SEQ_BLOCK)

    relb_flat = rel_bias.astype(F32).reshape(-1)
    att = [_dil_attn(q1.reshape(batch, 1, seq, 3 * D_ATT), relb_flat, 1),
           _dil_attn(q4, relb_flat, 4),
           _dil_attn(q16, relb_flat, 16)]

    kv = _mem_kv(mem, mem_norm[None, :], w_mem_kv.astype(BF16), bd, mkg)

    return _out_proj(x, conv_t, nrm, x0, att, qm, kv, gs, w_out.astype(BF16))


def kernel(x_prompt, x_sample, mem_prompt, mem_sample, norm_in, w_in, hy_conv_w, hy_conv_b, hy_filt_w1, hy_filt_b1, hy_filt_freq, hy_filt_w2, hy_filt_b2, hy_filt_w3, hy_skip, att_q_norm, att_k_norm, mem_norm, w_mem_kv, mem_q_norm, mem_k_norm, w_out, rel_bias):
    def trunk(x, mem):
        for l in range(norm_in.shape[0]):
            x = _encoder_layer(x, mem, norm_in[l], w_in[l], hy_conv_w[l], hy_conv_b[l],
                               hy_filt_w1[l], hy_filt_b1[l], hy_filt_freq[l], hy_filt_w2[l],
                               hy_filt_b2[l], hy_filt_w3[l], hy_skip[l], att_q_norm[l],
                               att_k_norm[l], mem_norm[l], w_mem_kv[l], mem_q_norm[l],
                               mem_k_norm[l], w_out[l], rel_bias)
        return x

    return (trunk(x_prompt, mem_prompt), trunk(x_sample, mem_sample))
```
